```python
import math
import jax, jax.numpy as jnp
from jax import lax
import numpy as np

D_MODEL = 1024
BATCH = 4
SEQ = 4096
DEPTH = 2
DEC_BATCH = 32
DEC_SEQ = 8
PAST_LEN = 16384
PAGE_SIZE = 128

H_FOX = 8
DH_FOX = 64
FOX_W = H_FOX * DH_FOX
H_GDN = 4
DK_GDN = 128
DV_GDN = 128
GDN_W = H_GDN * DV_GDN
CONV_W = 4
GDN_CHUNK = 64
H_MEM = 4
DH_MEM = 128
MEM_W = H_MEM * DH_MEM
N_MEM = 256
N_BRANCH = 3
Q_BLOCK = 128
EPS = 1e-6
IN_SPLITS = (FOX_W, FOX_W, FOX_W, FOX_W, H_FOX, 3 * GDN_W, GDN_W, H_GDN, H_GDN, MEM_W, MEM_W, N_BRANCH * D_MODEL)
IN_COLS = sum(IN_SPLITS)

kernel_name = 'fox_gdn_memory_gated_hybrid_step'


def rmsnorm(x, g):
    xf = x.astype(jnp.float32)
    y = xf * lax.rsqrt(jnp.mean(xf * xf, axis=-1, keepdims=True) + EPS)
    return (y * g.astype(jnp.float32)).astype(x.dtype)


def l2norm(x):
    xf = x.astype(jnp.float32)
    return xf * lax.rsqrt(jnp.sum(xf * xf, axis=-1, keepdims=True) + EPS)


def split_cols(x, sizes):
    offs = np.cumsum(np.array(sizes))[:-1].tolist()
    return jnp.split(x, offs, axis=-1)


def fox_attention(q, k, v, c, q_off):
    B, T, H, D = q.shape
    L = k.shape[1]
    blk = Q_BLOCK if T % Q_BLOCK == 0 else T
    nb = T // blk
    cT = jnp.transpose(c.astype(jnp.float32), (0, 2, 1))
    c_q = cT[:, :, q_off:q_off + T]
    qb = q.reshape(B, nb, blk, H, D).swapaxes(0, 1)
    cqb = c_q.reshape(B, H, nb, blk).transpose(2, 0, 1, 3)
    kpos = jnp.arange(L)
    scale = DH_FOX ** -0.5

    def one_block(args):
        i, qi, ci = args
        s = jnp.einsum('bqhd,bkhd->bhqk', qi, k, preferred_element_type=jnp.float32) * scale
        s = s + ci[..., None] - cT[:, :, None, :]
        qpos = q_off + i * blk + jnp.arange(blk)
        s = jnp.where(kpos[None, :] <= qpos[:, None], s, -jnp.inf)
        p = jax.nn.softmax(s, axis=-1).astype(v.dtype)
        return jnp.einsum('bhqk,bkhd->bqhd', p, v)

    o = lax.map(one_block, (jnp.arange(nb), qb, cqb))
    return o.swapaxes(0, 1).reshape(B, T, H, D)


def chunk_gated_delta(q, k, v, g, beta, S0):
    B, T, H, DK = q.shape
    DV = v.shape[-1]
    C = min(GDN_CHUNK, T)
    n = -(-T // C)
    pad = n * C - T

    def chunks(a):
        a = jnp.pad(a.astype(jnp.float32), [(0, 0), (0, pad)] + [(0, 0)] * (a.ndim - 2))
        a = a.reshape((B, n, C) + a.shape[2:])
        return jnp.moveaxis(a, 3, 2).swapaxes(0, 1)

    qc, kc, vc, gc, bc = chunks(q), chunks(k), chunks(v), chunks(g), chunks(beta)
    gc = jnp.cumsum(gc, axis=-1)
    idx = jnp.arange(C)
    tril = idx[:, None] >= idx[None, :]
    strict = idx[:, None] > idx[None, :]
    decay = jnp.exp(jnp.where(tril, gc[..., :, None] - gc[..., None, :], -jnp.inf))
    kb = kc * bc[..., None]
    M = jnp.where(strict, jnp.einsum('nbhid,nbhjd->nbhij', kb, kc) * decay, 0.0)
    A = M + jnp.eye(C, dtype=jnp.float32)
    rhs = jnp.concatenate([vc * bc[..., None], kb * jnp.exp(gc)[..., None]], axis=-1)
    sol = lax.linalg.triangular_solve(A, rhs, left_side=True, lower=True, unit_diagonal=True)
    u, w = sol[..., :DV], sol[..., DV:]
    qk = jnp.einsum('nbhid,nbhjd->nbhij', qc, kc) * decay

    def step(S, xs):
        q_i, k_i, u_i, w_i, g_i, qk_i = xs
        delta = u_i - jnp.einsum('bhcd,bhde->bhce', w_i, S)
        o = (jnp.einsum('bhcd,bhde->bhce', q_i * jnp.exp(g_i)[..., None], S)
             + jnp.einsum('bhij,bhje->bhie', qk_i, delta))
        g_last = g_i[..., -1:]
        S = (S * jnp.exp(g_last)[..., None]
             + jnp.einsum('bhcd,bhce->bhde', k_i * jnp.exp(g_last - g_i)[..., None], delta))
        return S, o

    S, o = lax.scan(step, S0.astype(jnp.float32), (qc, kc, u, w, gc, qk))
    o = jnp.moveaxis(o.swapaxes(0, 1), 2, 3).reshape(B, n * C, H, DV)[:, :T]
    return o, S


def memory_kv(mem, lp):
    B, M, _ = mem.shape
    m = rmsnorm(mem, lp['mem_norm_gain'])
    kv = jnp.einsum('bmd,de->bme', m, lp['w_mem_kv'])
    mk, mv = jnp.split(kv, 2, axis=-1)
    mk = rmsnorm(mk.reshape(B, M, H_MEM, DH_MEM), lp['mem_k_gain'])
    return mk, mv.reshape(B, M, H_MEM, DH_MEM)


def mixer_layer(x, lp, mem_k, mem_v, fox_past, conv_state, S0, q_off):
    B, T, _ = x.shape
    h = rmsnorm(x, lp['ln_gain'])
    proj = jnp.einsum('btd,de->bte', h, lp['w_in'])
    fq, fk, fv, fz, ff, gqkv, gz, ga, gb, mq, mz, gate_logits = split_cols(proj, IN_SPLITS)

    fq = rmsnorm(fq.reshape(B, T, H_FOX, DH_FOX), lp['fox_q_gain'])
    fk = rmsnorm(fk.reshape(B, T, H_FOX, DH_FOX), lp['fox_k_gain'])
    fv = fv.reshape(B, T, H_FOX, DH_FOX)
    logf = jax.nn.log_sigmoid((ff + lp['fox_f_bias']).astype(jnp.float32))
    if fox_past is None:
        k_all, v_all, lf_all = fk, fv, logf
    else:
        k_past, v_past, lf_past = fox_past
        k_all = jnp.concatenate([k_past.astype(fk.dtype), fk], axis=1)
        v_all = jnp.concatenate([v_past.astype(fv.dtype), fv], axis=1)
        lf_all = jnp.concatenate([lf_past.astype(jnp.float32), logf], axis=1)
    c = jnp.cumsum(lf_all, axis=1)
    o_fox = fox_attention(fq, k_all, v_all, c, q_off).reshape(B, T, FOX_W) * jax.nn.silu(fz)

    xpad = jnp.concatenate([conv_state.astype(gqkv.dtype), gqkv], axis=1)
    conv = sum(xpad[:, i:i + T] * lp['gdn_conv_w'][i] for i in range(CONV_W))
    new_conv = xpad[:, T:]
    gq, gk, gv = jnp.split(jax.nn.silu(conv), 3, axis=-1)
    gq = l2norm(gq.reshape(B, T, H_GDN, DK_GDN)) * DK_GDN ** -0.5
    gk = l2norm(gk.reshape(B, T, H_GDN, DK_GDN))
    gv = gv.reshape(B, T, H_GDN, DV_GDN)
    g = -jnp.exp(lp['gdn_A_log'].astype(jnp.float32)) * jax.nn.softplus((ga + lp['gdn_dt_bias']).astype(jnp.float32))
    beta = jax.nn.sigmoid(gb.astype(jnp.float32))
    o_gdn, S_new = chunk_gated_delta(gq, gk, gv, g, beta, S0)
    o_gdn = rmsnorm(o_gdn, lp['gdn_out_gain']).reshape(B, T, GDN_W).astype(x.dtype) * jax.nn.silu(gz)

    mq = rmsnorm(mq.reshape(B, T, H_MEM, DH_MEM), lp['mem_q_gain'])
    s = jnp.einsum('bthd,bmhd->bhtm', mq, mem_k.astype(mq.dtype), preferred_element_type=jnp.float32) * DH_MEM ** -0.5
    p = jax.nn.softmax(s, axis=-1).astype(x.dtype)
    o_mem = jnp.einsum('bhtm,bmhd->bthd', p, mem_v.astype(x.dtype)).reshape(B, T, MEM_W) * jax.nn.silu(mz)

    gates = jax.nn.sigmoid(gate_logits).reshape(B, T, N_BRANCH, D_MODEL)
    merged = (gates[:, :, 0] * jnp.einsum('btc,cd->btd', o_fox, lp['w_fox_br'])
              + gates[:, :, 1] * jnp.einsum('btc,cd->btd', o_gdn, lp['w_gdn_br'])
              + gates[:, :, 2] * jnp.einsum('btc,cd->btd', o_mem, lp['w_mem_br']))
    y = x + jnp.einsum('btd,de->bte', merged, lp['w_out'])
    return y, fk, fv, logf, new_conv, S_new


def setup_inputs(seed: int = 0) -> dict:
    key = jax.random.key(seed)
    ks = jax.random.split(key, 32)
    f32 = jnp.float32
    n_pages = PAST_LEN // PAGE_SIZE
    n_used = DEC_BATCH * n_pages
    n_pool = n_used + n_used // 4

    def nrm(k, shape, scale):
        return jax.random.normal(k, shape, f32) * scale

    dt = jnp.exp(jax.random.uniform(ks[17], (DEPTH, H_GDN), f32, math.log(1e-3), math.log(1e-1)))
    return {
        'x_prompt': nrm(ks[0], (BATCH, SEQ, D_MODEL), 1.0),
        'x_sample': nrm(ks[1], (DEC_BATCH, DEC_SEQ, D_MODEL), 1.0),
        'cache_fox_k': nrm(ks[2], (DEPTH, n_pool, PAGE_SIZE, H_FOX, DH_FOX), 1.0),
        'cache_fox_v': nrm(ks[3], (DEPTH, n_pool, PAGE_SIZE, H_FOX, DH_FOX), 1.0),
        'cache_fox_logf': jax.nn.log_sigmoid(3.0 + nrm(ks[4], (DEPTH, n_pool, PAGE_SIZE, H_FOX), 1.0)),
        'state_gdn': nrm(ks[5], (DEPTH, DEC_BATCH, H_GDN, DK_GDN, DV_GDN), 0.1),
        'state_gdn_conv': nrm(ks[6], (DEPTH, DEC_BATCH, CONV_W - 1, 3 * GDN_W), 1.0),
        'cache_mem_k': nrm(ks[7], (DEPTH, DEC_BATCH, N_MEM, H_MEM, DH_MEM), 1.0),
        'cache_mem_v': nrm(ks[8], (DEPTH, DEC_BATCH, N_MEM, H_MEM, DH_MEM), 1.0),
        'page_table': jax.random.permutation(ks[9], n_pool)[:n_used].reshape(DEC_BATCH, n_pages).astype(jnp.int32),
        'mem_prompt': nrm(ks[10], (BATCH, N_MEM, D_MODEL), 1.0),
        'ln_gain': 1.0 + nrm(ks[11], (DEPTH, D_MODEL), 0.02),
        'w_in': nrm(ks[12], (DEPTH, D_MODEL, IN_COLS), D_MODEL ** -0.5),
        'fox_q_gain': 1.0 + nrm(ks[13], (DEPTH, DH_FOX), 0.02),
        'fox_k_gain': 1.0 + nrm(ks[14], (DEPTH, DH_FOX), 0.02),
        'fox_f_bias': 3.0 + nrm(ks[15], (DEPTH, H_FOX), 0.1),
        'gdn_conv_w': nrm(ks[16], (DEPTH, CONV_W, 3 * GDN_W), CONV_W ** -0.5),
        'gdn_A_log': jnp.log(jax.random.uniform(ks[18], (DEPTH, H_GDN), f32, 1.0, 16.0)),
        'gdn_dt_bias': dt + jnp.log(-jnp.expm1(-dt)),
        'gdn_out_gain': 1.0 + nrm(ks[19], (DEPTH, DV_GDN), 0.02),
        'mem_norm_gain': 1.0 + nrm(ks[20], (DEPTH, D_MODEL), 0.02),
        'w_mem_kv': nrm(ks[21], (DEPTH, D_MODEL, 2 * MEM_W), D_MODEL ** -0.5),
        'mem_q_gain': 1.0 + nrm(ks[22], (DEPTH, DH_MEM), 0.02),
        'mem_k_gain': 1.0 + nrm(ks[23], (DEPTH, DH_MEM), 0.02),
        'w_fox_br': nrm(ks[24], (DEPTH, FOX_W, D_MODEL), FOX_W ** -0.5),
        'w_gdn_br': nrm(ks[25], (DEPTH, GDN_W, D_MODEL), GDN_W ** -0.5),
        'w_mem_br': nrm(ks[26], (DEPTH, MEM_W, D_MODEL), MEM_W ** -0.5),
        'w_out': nrm(ks[27], (DEPTH, D_MODEL, D_MODEL), D_MODEL ** -0.5),
    }


def reference(x_prompt, x_sample, cache_fox_k, cache_fox_v, cache_fox_logf, state_gdn, state_gdn_conv,
              cache_mem_k, cache_mem_v, page_table, mem_prompt, ln_gain, w_in, fox_q_gain, fox_k_gain,
              fox_f_bias, gdn_conv_w, gdn_A_log, gdn_dt_bias, gdn_out_gain, mem_norm_gain, w_mem_kv,
              mem_q_gain, mem_k_gain, w_fox_br, w_gdn_br, w_mem_br, w_out):
    B = x_prompt.shape[0]
    Bd = x_sample.shape[0]
    past_len = page_table.shape[1] * PAGE_SIZE
    yp, ys = x_prompt, x_sample
    pk_l, pv_l, plf_l, pS_l, pconv_l, pmk_l, pmv_l = [], [], [], [], [], [], []
    sk_l, sv_l, slf_l, sS_l, sconv_l = [], [], [], [], []
    for l in range(DEPTH):
        lp = dict(ln_gain=ln_gain[l], w_in=w_in[l], fox_q_gain=fox_q_gain[l], fox_k_gain=fox_k_gain[l],
                  fox_f_bias=fox_f_bias[l], gdn_conv_w=gdn_conv_w[l], gdn_A_log=gdn_A_log[l],
                  gdn_dt_bias=gdn_dt_bias[l], gdn_out_gain=gdn_out_gain[l], mem_norm_gain=mem_norm_gain[l],
                  w_mem_kv=w_mem_kv[l], mem_q_gain=mem_q_gain[l], mem_k_gain=mem_k_gain[l],
                  w_fox_br=w_fox_br[l], w_gdn_br=w_gdn_br[l], w_mem_br=w_mem_br[l], w_out=w_out[l])
        mk, mv = memory_kv(mem_prompt, lp)
        conv0 = jnp.zeros((B, CONV_W - 1, 3 * GDN_W), x_prompt.dtype)
        S0 = jnp.zeros((B, H_GDN, DK_GDN, DV_GDN), jnp.float32)
        yp, pk, pv, plf, pconv, pS = mixer_layer(yp, lp, mk, mv, None, conv0, S0, 0)
        pk_l.append(pk); pv_l.append(pv); plf_l.append(plf); pS_l.append(pS); pconv_l.append(pconv)
        pmk_l.append(mk); pmv_l.append(mv)
        k_past = cache_fox_k[l, page_table].reshape(Bd, past_len, H_FOX, DH_FOX)
        v_past = cache_fox_v[l, page_table].reshape(Bd, past_len, H_FOX, DH_FOX)
        lf_past = cache_fox_logf[l, page_table].reshape(Bd, past_len, H_FOX)
        ys, sk, sv, slf, sconv, sS = mixer_layer(ys, lp, cache_mem_k[l], cache_mem_v[l],
                                                 (k_past, v_past, lf_past), state_gdn_conv[l], state_gdn[l],
                                                 past_len)
        sk_l.append(sk); sv_l.append(sv); slf_l.append(slf); sS_l.append(sS); sconv_l.append(sconv)
    return (yp, ys,
            jnp.stack(pk_l), jnp.stack(pv_l), jnp.stack(plf_l), jnp.stack(pS_l), jnp.stack(pconv_l),
            jnp.stack(pmk_l), jnp.stack(pmv_l),
            jnp.stack(sk_l), jnp.stack(sv_l), jnp.stack(slf_l), jnp.stack(sS_l), jnp.stack(sconv_l))
```

```python
import functools

import numpy as np
import jax
import jax.numpy as jnp
from jax import lax
from jax.experimental import pallas as pl
from jax.experimental.pallas import tpu as pltpu

F32, BF16 = jnp.float32, jnp.bfloat16

H_FOX, DH_FOX = 8, 64
FOX_W = H_FOX * DH_FOX
H_GDN, DK_GDN, DV_GDN = 4, 128, 128
GDN_W = H_GDN * DV_GDN
CONV_W = 4
GDN_CHUNK = 64
H_MEM, DH_MEM = 4, 128
MEM_W = H_MEM * DH_MEM
N_BRANCH = 3
EPS = 1e-6
NEG = -1e30
LANES = 128
VMEM_LIMIT = 56 * 1024 * 1024

FQ, FK, FV, FZ, GQ, GK, GV, GZ, MQ, MZ = range(10)
GATE0 = 5
SM_FF, SM_GA, SM_GB = 0, 8, 12


def _params(*sem):
    return pltpu.CompilerParams(dimension_semantics=sem, vmem_limit_bytes=VMEM_LIMIT)


def _dot(a, b):
    return jnp.dot(a, b, preferred_element_type=F32)


def _dot_nt(a, b):
    return lax.dot_general(a, b, (((1,), (1,)), ((), ())), preferred_element_type=F32)


def _dot_tn(a, b):
    return lax.dot_general(a, b, (((0,), (0,)), ((), ())), preferred_element_type=F32)


def _split2(x):
    hi = x.astype(BF16)
    lo = (x - hi.astype(F32)).astype(BF16)
    return hi, lo


def _split3(x):
    hi = x.astype(BF16)
    r = x - hi.astype(F32)
    mid = r.astype(BF16)
    lo = (r - mid.astype(F32)).astype(BF16)
    return hi, mid, lo


def _dot3_lhs_exact(m, x):
    hi, mid, lo = _split3(x)
    return _dot(m, hi) + _dot(m, mid) + _dot(m, lo)


def _dot3_rhs_exact(x, m):
    hi, mid, lo = _split3(x)
    return _dot(hi, m) + _dot(mid, m) + _dot(lo, m)


def _sigmoid(x):
    return 1.0 / (1.0 + jnp.exp(-x))


def _silu(x):
    return x * _sigmoid(x)


def _softplus(x):
    return jnp.maximum(x, 0.0) + jnp.log1p(jnp.exp(-jnp.abs(x)))


def _log_sigmoid(x):
    return jnp.minimum(x, 0.0) - jnp.log1p(jnp.exp(-jnp.abs(x)))


def _lane_bcast(x, lane):
    return jnp.broadcast_to(x[:, lane:lane + 1], x.shape)


def _headnorm128(x, gain, nheads):
    outs = []
    for h in range(nheads):
        seg = x[:, h * LANES:(h + 1) * LANES]
        ms = jnp.mean(seg * seg, axis=-1, keepdims=True)
        outs.append(seg * lax.rsqrt(ms + EPS) * gain)
    return jnp.concatenate(outs, axis=1)


def _in_proj_kernel(x_ref, g_ref, w_ref, ws_ref, p_ref, ps_ref, h_ref):
    @pl.when(pl.program_id(1) == 0)
    def _():
        x = x_ref[...]
        ms = jnp.mean(x * x, axis=-1, keepdims=True)
        h = (x * lax.rsqrt(ms + EPS) * g_ref[...]).astype(BF16)
        h_ref[...] = h
        ps_ref[...] = _dot(h, ws_ref[...])

    p_ref[...] = _dot(h_ref[...], w_ref[...])


def _in_proj(x, gain, w_big, w_small, tm):
    n, d = x.shape
    nb = w_big.shape[1]
    tn = 1024
    return pl.pallas_call(
        _in_proj_kernel,
        grid=(n // tm, nb // tn),
        in_specs=[pl.BlockSpec((tm, d), lambda i, j: (i, 0)),
                  pl.BlockSpec((1, d), lambda i, j: (0, 0)),
                  pl.BlockSpec((d, tn), lambda i, j: (0, j)),
                  pl.BlockSpec((d, LANES), lambda i, j: (0, 0))],
        out_specs=[pl.BlockSpec((tm, tn), lambda i, j: (i, j)),
                   pl.BlockSpec((tm, LANES), lambda i, j: (i, 0))],
        out_shape=[jax.ShapeDtypeStruct((n, nb), F32), jax.ShapeDtypeStruct((n, LANES), F32)],
        scratch_shapes=[pltpu.VMEM((tm, d), BF16)],
        compiler_params=_params("parallel", "arbitrary"),
        name="in_proj")(x, gain, w_big, w_small)


def _fox_headnorm(x, gain, bd):
    hi, lo = _split2(x * x)
    ss = _dot(hi, bd) + _dot(lo, bd)
    return x * lax.rsqrt(ss * (1.0 / DH_FOX) + EPS) * gain


def _fox_prep_kernel(fq_ref, fk_ref, fv_ref, ps_ref, qg_ref, kg_ref, fb_ref, bd_ref, tri_ref,
                     selq_ref, selk_ref, selv_ref, qrow_ref, krow_ref,
                     kn_ref, v_ref, lf_ref, qa_ref, ka_ref, va_ref, carry_ref):
    @pl.when(pl.program_id(1) == 0)
    def _():
        carry_ref[...] = jnp.zeros_like(carry_ref)

    bd = bd_ref[...]
    qn = _fox_headnorm(fq_ref[...], qg_ref[...], bd) * (DH_FOX ** -0.5)
    kn = _fox_headnorm(fk_ref[...], kg_ref[...], bd)
    fv = fv_ref[...]
    kn_ref[0] = kn
    v_ref[0] = fv
    lf = _log_sigmoid(ps_ref[...] + fb_ref[...])
    lf_ref[0] = lf
    c = _dot3_lhs_exact(tri_ref[...], lf) + carry_ref[0:1, :]
    tm = c.shape[0]
    carry_ref[...] = jnp.broadcast_to(c[tm - 1:tm, :], carry_ref.shape)
    chi, cmid, clo = _split3(c)
    qa = _dot(jnp.concatenate([qn.astype(BF16), chi, cmid, clo], axis=1), selq_ref[...]) + qrow_ref[...]
    ka = _dot(jnp.concatenate([kn.astype(BF16), chi, cmid, clo], axis=1), selk_ref[...]) + krow_ref[...]
    qa_ref[0] = qa.astype(BF16)
    ka_ref[0] = ka.astype(BF16)
    va_ref[0] = _dot(fv.astype(BF16), selv_ref[...]).astype(BF16)


def _fox_consts(tm):
    a = np.arange(FOX_W)
    bd = (a[:, None] // DH_FOX == a[None, :] // DH_FOX).astype(np.float32)
    r = np.arange(tm)
    tri = (r[None, :] <= r[:, None]).astype(np.float32)
    aw = H_FOX * LANES
    selq = np.zeros((FOX_W + 3 * LANES, aw), np.float32)
    selk = np.zeros((FOX_W + 3 * LANES, aw), np.float32)
    selv = np.zeros((FOX_W, aw), np.float32)
    qrow = np.zeros((1, aw), np.float32)
    krow = np.zeros((1, aw), np.float32)
    for h in range(H_FOX):
        for d in range(DH_FOX):
            selq[h * DH_FOX + d, h * LANES + d] = 1.0
            selk[h * DH_FOX + d, h * LANES + d] = 1.0
            selv[h * DH_FOX + d, h * LANES + (h % 2) * DH_FOX + d] = 1.0
        for e in range(3):
            selq[FOX_W + e * LANES + h, h * LANES + DH_FOX + e] = 1.0
            selk[FOX_W + e * LANES + h, h * LANES + DH_FOX + 3 + e] = -1.0
            qrow[0, h * LANES + DH_FOX + 3 + e] = 1.0
            krow[0, h * LANES + DH_FOX + e] = 1.0
    bf = lambda z: jnp.asarray(z, BF16)
    return bf(bd), bf(tri), bf(selq), bf(selk), bf(selv), jnp.asarray(qrow), jnp.asarray(krow)


def _tile_gain(g, reps):
    return jnp.tile(g.astype(F32), reps)[None, :]


def _pad_row(vals, offset):
    row = jnp.zeros((LANES,), F32)
    return lax.dynamic_update_slice(row, vals.astype(F32), (offset,))[None, :]


def _fox_prep_prompt(p, ps, qgain, kgain, fbias, b, t):
    tm = min(512, t)
    nt = t // tm
    bd, tri, selq, selk, selv, qrow, krow = _fox_consts(tm)
    aw = H_FOX * LANES
    row = lambda blk: pl.BlockSpec((tm, FOX_W), lambda i, j, blk=blk: (i * nt + j, blk))
    const = lambda shape: pl.BlockSpec(shape, lambda i, j: (0,) * len(shape))
    tok = lambda w: pl.BlockSpec((1, tm, w), lambda i, j: (i, j, 0))
    return pl.pallas_call(
        _fox_prep_kernel,
        grid=(b, nt),
        in_specs=[row(FQ), row(FK), row(FV),
                  pl.BlockSpec((tm, LANES), lambda i, j: (i * nt + j, 0)),
                  const((1, FOX_W)), const((1, FOX_W)), const((1, LANES)),
                  const(bd.shape), const(tri.shape), const(selq.shape), const(selk.shape), const(selv.shape),
                  const(qrow.shape), const(krow.shape)],
        out_specs=[tok(FOX_W), tok(FOX_W), tok(LANES), tok(aw), tok(aw), tok(aw)],
        out_shape=[jax.ShapeDtypeStruct((b, t, FOX_W), F32), jax.ShapeDtypeStruct((b, t, FOX_W), F32),
                   jax.ShapeDtypeStruct((b, t, LANES), F32),
                   jax.ShapeDtypeStruct((b, t, aw), BF16), jax.ShapeDtypeStruct((b, t, aw), BF16),
                   jax.ShapeDtypeStruct((b, t, aw), BF16)],
        scratch_shapes=[pltpu.VMEM((8, LANES), F32)],
        compiler_params=_params("parallel", "arbitrary"),
        name="fox_prep")(p, p, p, ps, _tile_gain(qgain, H_FOX), _tile_gain(kgain, H_FOX), _pad_row(fbias, SM_FF),
                         bd, tri, selq, selk, selv, qrow, krow)


def _fox_attn_kernel(q_ref, k_ref, v_ref, o_ref, m_ref, l_ref, acc_ref, *, tk):
    qi = pl.program_id(2)
    tq = q_ref.shape[1]
    row = lax.broadcasted_iota(jnp.int32, (tq, tk), 0)
    col = lax.broadcasted_iota(jnp.int32, (tq, tk), 1)
    out = None
    for hh in range(2):
        lanes = slice(hh * LANES, (hh + 1) * LANES)
        q = q_ref[0, :, lanes]
        m_ref[...] = jnp.full(m_ref.shape, NEG, F32)
        l_ref[...] = jnp.zeros_like(l_ref)
        acc_ref[...] = jnp.zeros_like(acc_ref)

        def step(ki, masked, lanes=lanes, q=q):
            r0 = pl.multiple_of(ki * tk, tk)
            k = k_ref[0, pl.ds(r0, tk), lanes]
            v = v_ref[0, pl.ds(r0, tk), lanes]
            s = _dot_nt(q, k)
            if masked:
                s = jnp.where(col <= row, s, NEG)
            m_prev = m_ref[...]
            m_new = jnp.maximum(m_prev, jnp.max(s, axis=-1, keepdims=True))
            alpha = jnp.exp(m_prev - m_new)
            p = jnp.exp(s - m_new)
            l_ref[...] = alpha * l_ref[...] + jnp.sum(p, axis=-1, keepdims=True)
            acc_ref[...] = alpha * acc_ref[...] + _dot(p.astype(BF16), v)
            m_ref[...] = m_new

        def body(ki, carry, step=step):
            step(ki, False)
            return carry

        lax.fori_loop(0, qi, body, 0)
        step(qi, True)
        o = acc_ref[...] / l_ref[...]
        out = o if out is None else out + o
    o_ref[0] = out


def _fox_attn_prompt(qa, ka, va, b, t):
    tq = min(512, t)
    pair = 2 * LANES
    return pl.pallas_call(
        functools.partial(_fox_attn_kernel, tk=tq),
        grid=(b, H_FOX // 2, t // tq),
        in_specs=[pl.BlockSpec((1, tq, pair), lambda i, h, q: (i, q, h)),
                  pl.BlockSpec((1, t, pair), lambda i, h, q: (i, 0, h)),
                  pl.BlockSpec((1, t, pair), lambda i, h, q: (i, 0, h))],
        out_specs=pl.BlockSpec((1, tq, LANES), lambda i, h, q: (i, q, h)),
        out_shape=jax.ShapeDtypeStruct((b, t, FOX_W), F32),
        scratch_shapes=[pltpu.VMEM((tq, 1), F32), pltpu.VMEM((tq, 1), F32), pltpu.VMEM((tq, LANES), F32)],
        compiler_params=_params("parallel", "parallel", "arbitrary"),
        name="fox_attn")(qa, ka, va)


def _fox_sample_prep_kernel(fq_ref, fk_ref, ps_ref, qg_ref, kg_ref, fb_ref, bd_ref, tri_ref,
                            qn_ref, kn_ref, lf_ref, cn_ref):
    bd = bd_ref[...]
    qn_ref[...] = _fox_headnorm(fq_ref[...], qg_ref[...], bd) * (DH_FOX ** -0.5)
    kn_ref[...] = _fox_headnorm(fk_ref[...], kg_ref[...], bd)
    lf = _log_sigmoid(ps_ref[...] + fb_ref[...])
    lf_ref[...] = lf
    cn_ref[...] = _dot3_lhs_exact(tri_ref[...], lf)


def _fox_sample_prep(p, ps, qgain, kgain, fbias, td):
    n = p.shape[0]
    a = np.arange(FOX_W)
    bd = jnp.asarray((a[:, None] // DH_FOX == a[None, :] // DH_FOX), BF16)
    r = np.arange(n)
    tri = jnp.asarray((r[None, :] <= r[:, None]) & (r[None, :] // td == r[:, None] // td), BF16)
    full = lambda shape: pl.BlockSpec(shape, lambda i: (0,) * len(shape))
    return pl.pallas_call(
        _fox_sample_prep_kernel,
        grid=(1,),
        in_specs=[pl.BlockSpec((n, FOX_W), lambda i: (0, FQ)), pl.BlockSpec((n, FOX_W), lambda i: (0, FK)),
                  full((n, LANES)), full((1, FOX_W)), full((1, FOX_W)), full((1, LANES)),
                  full(bd.shape), full(tri.shape)],
        out_specs=[full((n, FOX_W)), full((n, FOX_W)), full((n, LANES)), full((n, LANES))],
        out_shape=[jax.ShapeDtypeStruct((n, FOX_W), F32), jax.ShapeDtypeStruct((n, FOX_W), F32),
                   jax.ShapeDtypeStruct((n, LANES), F32), jax.ShapeDtypeStruct((n, LANES), F32)],
        compiler_params=_params("arbitrary"),
        name="fox_sample_prep")(p, p, ps, _tile_gain(qgain, H_FOX), _tile_gain(kgain, H_FOX),
                                _pad_row(fbias, SM_FF), bd, tri)


def _page_logf_kernel(x_ref, u_ref, ones_ref, r_ref, s_ref):
    x = x_ref[...]
    r_ref[...] = _dot3_rhs_exact(x, u_ref[...])
    s_ref[...] = _dot3_rhs_exact(x, ones_ref[...])


def _page_logf(xt):
    n, page = xt.shape
    tm = 2048
    while n % tm:
        tm //= 2
    j = np.arange(page)
    u = jnp.asarray(j[:, None] > j[None, :], BF16)
    ones = jnp.ones((page, page), BF16)
    return pl.pallas_call(
        _page_logf_kernel,
        grid=(n // tm,),
        in_specs=[pl.BlockSpec((tm, page), lambda i: (i, 0)),
                  pl.BlockSpec((page, page), lambda i: (0, 0)), pl.BlockSpec((page, page), lambda i: (0, 0))],
        out_specs=[pl.BlockSpec((tm, page), lambda i: (i, 0)), pl.BlockSpec((tm, page), lambda i: (i, 0))],
        out_shape=[jax.ShapeDtypeStruct((n, page), F32), jax.ShapeDtypeStruct((n, page), F32)],
        compiler_params=_params("parallel"),
        name="page_logf")(xt, u, ones)


def _fox_decode_kernel(pidx_ref, q_ref, cn_ref, cnk_ref, knew_ref, vnew_ref, *rest, g, td):
    k_refs, v_refs, r_refs, s_refs = rest[0:g], rest[g:2 * g], rest[2 * g:3 * g], rest[3 * g:4 * g]
    o_ref, m_ref, l_ref, acc_ref, suf_ref = rest[4 * g:]
    p = pl.program_id(1)
    q = q_ref[0]
    cn = cn_ref[0]
    nrow = q.shape[0]

    @pl.when(p == 0)
    def _():
        s = _dot_nt(q, knew_ref[0])
        qpos = lax.broadcasted_iota(jnp.int32, s.shape, 0) % td
        kpos = lax.broadcasted_iota(jnp.int32, s.shape, 1)
        s = jnp.where(kpos <= qpos, s + cn - cnk_ref[0], NEG)
        m = jnp.max(s, axis=-1, keepdims=True)
        pr = jnp.exp(s - m)
        m_ref[...] = m
        l_ref[...] = jnp.sum(pr, axis=-1, keepdims=True)
        acc_ref[...] = _dot(pr.astype(BF16), vnew_ref[0])
        suf_ref[...] = jnp.zeros_like(suf_ref)

    suf = suf_ref[...]
    scores = []
    for j in range(g):
        z = r_refs[j][0] + suf
        bias = jnp.concatenate([jnp.broadcast_to(z[h:h + 1, :], (td, LANES)) for h in range(H_FOX)], axis=0)
        scores.append(_dot_nt(q, k_refs[j][0].astype(BF16)) + bias + cn)
        suf = suf + s_refs[j][0]
    suf_ref[...] = suf
    m_prev = m_ref[...]
    m_new = m_prev
    for s in scores:
        m_new = jnp.maximum(m_new, jnp.max(s, axis=-1, keepdims=True))
    alpha = jnp.exp(m_prev - m_new)
    l = alpha * l_ref[...]
    acc = alpha * acc_ref[...]
    for j in range(g):
        pr = jnp.exp(scores[j] - m_new)
        l = l + jnp.sum(pr, axis=-1, keepdims=True)
        acc = acc + _dot(pr.astype(BF16), v_refs[j][0].astype(BF16))
    m_ref[...] = m_new
    l_ref[...] = l
    acc_ref[...] = acc

    @pl.when(p == pl.num_programs(1) - 1)
    def _():
        o = acc / l
        head = lax.broadcasted_iota(jnp.int32, (td, FOX_W), 1) // DH_FOX
        out = jnp.zeros((td, FOX_W), F32)
        for h in range(H_FOX):
            out = out + jnp.where(head == h, o[h * td:(h + 1) * td, :], 0.0)
        o_ref[0] = out


def _fox_decode(pidx, qbd, cn, cnk, knew, vnew, kc, vc, r3, s3, n_pages, td):
    bd_, nrow, _ = qbd.shape
    page = kc.shape[1]
    g = min(8, n_pages)
    steps = n_pages // g

    def page_map(j):
        return lambda b, p, idx: (idx[b * n_pages + (n_pages - 1 - (p * g + j))], 0, 0)

    per_b = lambda shape: pl.BlockSpec((1,) + shape, lambda b, p, idx: (b, 0, 0))
    in_specs = [per_b((nrow, FOX_W)), per_b((nrow, LANES)), per_b((nrow, LANES)),
                per_b((page, FOX_W)), per_b((page, FOX_W))]
    in_specs += [pl.BlockSpec((1, page, FOX_W), page_map(j)) for j in range(g)]
    in_specs += [pl.BlockSpec((1, page, FOX_W), page_map(j)) for j in range(g)]
    in_specs += [pl.BlockSpec((1, H_FOX, LANES), page_map(j)) for j in range(g)]
    in_specs += [pl.BlockSpec((1, H_FOX, LANES), page_map(j)) for j in range(g)]
    grid_spec = pltpu.PrefetchScalarGridSpec(
        num_scalar_prefetch=1, grid=(bd_, steps), in_specs=in_specs,
        out_specs=pl.BlockSpec((1, td, FOX_W), lambda b, p, idx: (b, 0, 0)),
        scratch_shapes=[pltpu.VMEM((nrow, 1), F32), pltpu.VMEM((nrow, 1), F32), pltpu.VMEM((nrow, FOX_W), F32),
                        pltpu.VMEM((H_FOX, LANES), F32)])
    return pl.pallas_call(
        functools.partial(_fox_decode_kernel, g=g, td=td),
        grid_spec=grid_spec,
        out_shape=jax.ShapeDtypeStruct((bd_, td, FOX_W), F32),
        compiler_params=_params("parallel", "arbitrary"),
        name="fox_decode")(pidx, qbd, cn, cnk, knew, vnew, *([kc] * g), *([vc] * g), *([r3] * g), *([s3] * g))


def _gdn_prep_kernel(gq_ref, gk_ref, gv_ref, ps_ref, init_ref, w_ref, alog_ref, dtb_ref,
                     q_ref, k_ref, v_ref, g_ref, ext_ref):
    tm = gq_ref.shape[0]

    @pl.when(pl.program_id(1) == 0)
    def _():
        ext_ref[0:8, :] = init_ref[0]

    @pl.when(pl.program_id(1) > 0)
    def _():
        ext_ref[0:8, :] = ext_ref[tm:tm + 8, :]

    ext_ref[8:8 + tm, 0:GDN_W] = gq_ref[...]
    ext_ref[8:8 + tm, GDN_W:2 * GDN_W] = gk_ref[...]
    ext_ref[8:8 + tm, 2 * GDN_W:3 * GDN_W] = gv_ref[...]
    conv = ext_ref[8:8 + tm, :] * w_ref[CONV_W - 1:CONV_W, :]
    for i in range(CONV_W - 1):
        off = 8 - (CONV_W - 1) + i
        conv = conv + ext_ref[off:off + tm, :] * w_ref[i:i + 1, :]
    s = _silu(conv)

    def l2(x):
        outs = []
        for h in range(H_GDN):
            seg = x[:, h * LANES:(h + 1) * LANES]
            outs.append(seg * lax.rsqrt(jnp.sum(seg * seg, axis=-1, keepdims=True) + EPS))
        return jnp.concatenate(outs, axis=1)

    q_ref[...] = l2(s[:, 0:GDN_W]) * (DK_GDN ** -0.5)
    k_ref[...] = l2(s[:, GDN_W:2 * GDN_W])
    v_ref[...] = s[:, 2 * GDN_W:3 * GDN_W]
    ps = ps_ref[...]
    lane = lax.broadcasted_iota(jnp.int32, ps.shape, 1)
    gate = -jnp.exp(alog_ref[...]) * _softplus(ps + dtb_ref[...])
    beta = _sigmoid(ps)
    g_ref[...] = jnp.where((lane >= SM_GA) & (lane < SM_GB), gate,
                           jnp.where((lane >= SM_GB) & (lane < SM_GB + H_GDN), beta, 0.0))


def _gdn_prep(p, ps, init, conv_w, a_log, dt_bias, b, t):
    tm = min(512, t)
    nt = t // tm
    n = b * t
    row = lambda blk: pl.BlockSpec((tm, GDN_W), lambda i, j, blk=blk: (i * nt + j, blk))
    out = pl.BlockSpec((tm, GDN_W), lambda i, j: (i * nt + j, 0))
    small = pl.BlockSpec((tm, LANES), lambda i, j: (i * nt + j, 0))
    const = lambda shape: pl.BlockSpec(shape, lambda i, j: (0,) * len(shape))
    return pl.pallas_call(
        _gdn_prep_kernel,
        grid=(b, nt),
        in_specs=[row(GQ), row(GK), row(GV), small,
                  pl.BlockSpec((1, 8, 3 * GDN_W), lambda i, j: (i, 0, 0)),
                  const((CONV_W, 3 * GDN_W)), const((1, LANES)), const((1, LANES))],
        out_specs=[out, out, out, small],
        out_shape=[jax.ShapeDtypeStruct((n, GDN_W), F32)] * 3 + [jax.ShapeDtypeStruct((n, LANES), F32)],
        scratch_shapes=[pltpu.VMEM((tm + 8, 3 * GDN_W), F32)],
        compiler_params=_params("parallel", "arbitrary"),
        name="gdn_prep")(p, p, p, ps, init, conv_w.astype(F32), _pad_row(a_log, SM_GA), _pad_row(dt_bias, SM_GA))


def _gdn_chunk_kernel(q_ref, k_ref, v_ref, g_ref, tri_ref,
                      m_ref, qk_ref, u_ref, w_ref, qg_ref, kg_ref, egl_ref, *, nch):
    c = GDN_CHUNK
    tri = tri_ref[...]
    ri = lax.broadcasted_iota(jnp.int32, (c, c), 0)
    ci = lax.broadcasted_iota(jnp.int32, (c, c), 1)
    lane = lax.broadcasted_iota(jnp.int32, (c, LANES), 1)
    for cc in range(nch):
        rows = slice(cc * c, (cc + 1) * c)
        gt = g_ref[rows, :]
        gcum = _dot3_lhs_exact(tri, gt)
        for h in range(H_GDN):
            cols = slice(h * LANES, (h + 1) * LANES)
            gcol = _lane_bcast(gcum, SM_GA + h)
            beta = _lane_bcast(gt, SM_GB + h)
            hi, mid, lo = [z.astype(F32) for z in _split3(gcol)]
            a = jnp.where(lane == 0, hi, jnp.where(lane == 1, mid, jnp.where(lane == 2, lo,
                          jnp.where(lane < 6, 1.0, 0.0))))
            bm = jnp.where(lane < 3, 1.0, jnp.where(lane == 3, -hi, jnp.where(lane == 4, -mid,
                           jnp.where(lane == 5, -lo, 0.0))))
            diff = _dot_nt(a.astype(BF16), bm.astype(BF16))
            decay = jnp.exp(jnp.where(ci <= ri, diff, NEG))
            qh, kh, vh = q_ref[rows, cols], k_ref[rows, cols], v_ref[rows, cols]
            kb = kh * beta
            khb = kh.astype(BF16)
            m_ref[cc * H_GDN + h] = jnp.where(ci < ri, _dot_nt(kb.astype(BF16), khb) * decay, 0.0)
            qk_ref[cc * H_GDN + h] = _dot_nt(qh.astype(BF16), khb) * decay
            eg = jnp.exp(gcol)
            glast = gcol[c - 1:c, :]
            u_ref[rows, cols] = vh * beta
            w_ref[rows, cols] = kb * eg
            qg_ref[rows, cols] = qh * eg
            kg_ref[rows, cols] = kh * jnp.exp(glast - gcol)
            egl_ref[cc * H_GDN + h] = jnp.broadcast_to(jnp.exp(glast), (8, LANES))


def _gdn_chunk(q, k, v, g):
    n = q.shape[0]
    c = GDN_CHUNK
    tm = min(256, n)
    nch = tm // c
    r = np.arange(c)
    tri = jnp.asarray(r[None, :] <= r[:, None], BF16)
    nc = n // c * H_GDN
    row = pl.BlockSpec((tm, GDN_W), lambda i: (i, 0))
    blk = lambda shape: pl.BlockSpec((nch * H_GDN,) + shape, lambda i: (i, 0, 0))
    return pl.pallas_call(
        functools.partial(_gdn_chunk_kernel, nch=nch),
        grid=(n // tm,),
        in_specs=[row, row, row, pl.BlockSpec((tm, LANES), lambda i: (i, 0)), pl.BlockSpec((c, c), lambda i: (0, 0))],
        out_specs=[blk((c, c)), blk((c, c)), row, row, row, row, blk((8, LANES))],
        out_shape=[jax.ShapeDtypeStruct((nc, c, c), F32), jax.ShapeDtypeStruct((nc, c, c), F32)]
                  + [jax.ShapeDtypeStruct((n, GDN_W), F32)] * 4 + [jax.ShapeDtypeStruct((nc, 8, LANES), F32)],
        compiler_params=_params("parallel"),
        name="gdn_chunk")(q, k, v, g, tri)


def _gdn_inv_kernel(m_ref, x_ref):
    c = GDN_CHUNK
    for i in range(c):
        ext = 8 * (i // 8 + 1)
        e_i = (lax.broadcasted_iota(jnp.int32, (ext, LANES), 0) == i).astype(F32)

        def body(j, acc, i=i, ext=ext):
            return acc - m_ref[i, pl.ds(j, 1), :] * x_ref[j, 0:ext, :]

        x_ref[i, 0:ext, :] = lax.fori_loop(0, i, body, e_i)
        if ext < c:
            x_ref[i, ext:c, :] = jnp.zeros((c - ext, LANES), F32)


def _gdn_inv(mt):
    c = GDN_CHUNK
    nc = mt.shape[2]
    spec = pl.BlockSpec((c, c, LANES), lambda i: (0, 0, i))
    return pl.pallas_call(
        _gdn_inv_kernel,
        grid=(nc // LANES,),
        in_specs=[spec], out_specs=spec,
        out_shape=jax.ShapeDtypeStruct((c, c, nc), F32),
        compiler_params=_params("parallel"),
        name="gdn_inv")(mt)


def _gdn_scan_kernel(x_ref, qk_ref, u_ref, w_ref, qg_ref, kg_ref, egl_ref, s0_ref, o_ref, sout_ref, s_ref, *, nch):
    c = GDN_CHUNK

    @pl.when(pl.program_id(1) == 0)
    def _():
        s_ref[...] = s0_ref[0]

    def body(cc, carry):
        r0 = pl.multiple_of(cc * c, c)
        for h in range(H_GDN):
            cols = slice(h * LANES, (h + 1) * LANES)
            xm = x_ref[cc * H_GDN + h]
            rhs = jnp.concatenate([u_ref[pl.ds(r0, c), cols], w_ref[pl.ds(r0, c), cols]], axis=1)
            xh, xl = _split2(xm)
            rh, rl = _split2(rhs)
            sol = _dot(xh, rh) + _dot(xh, rl) + _dot(xl, rh)
            u, w = sol[:, 0:DV_GDN], sol[:, DV_GDN:]
            st = s_ref[h]
            both = _dot(jnp.concatenate([w, qg_ref[pl.ds(r0, c), cols]], axis=0).astype(BF16), st.astype(BF16))
            delta = u - both[0:c]
            db = delta.astype(BF16)
            o_ref[pl.ds(r0, c), cols] = both[c:] + _dot(qk_ref[cc * H_GDN + h].astype(BF16), db)
            egl = jnp.broadcast_to(egl_ref[cc * H_GDN + h][0:1, :], (DK_GDN, DV_GDN))
            s_ref[h] = st * egl + _dot_tn(kg_ref[pl.ds(r0, c), cols].astype(BF16), db)
        return carry

    lax.fori_loop(0, nch, body, 0)

    @pl.when(pl.program_id(1) == pl.num_programs(1) - 1)
    def _():
        sout_ref[0] = s_ref[...]


def _gdn_scan(x, qk, u, w, qg, kg, egl, s0, b, t):
    c = GDN_CHUNK
    tg = min(1024, t)
    ng = t // tg
    nch = tg // c
    row = pl.BlockSpec((tg, GDN_W), lambda i, j: (i * ng + j, 0))
    blk = lambda shape: pl.BlockSpec((nch * H_GDN,) + shape, lambda i, j: (i * ng + j, 0, 0))
    st = pl.BlockSpec((1, H_GDN, DK_GDN, DV_GDN), lambda i, j: (i, 0, 0, 0))
    return pl.pallas_call(
        functools.partial(_gdn_scan_kernel, nch=nch),
        grid=(b, ng),
        in_specs=[blk((c, c)), blk((c, c)), row, row, row, row, blk((8, LANES)), st],
        out_specs=[row, st],
        out_shape=[jax.ShapeDtypeStruct((b * t, GDN_W), F32), jax.ShapeDtypeStruct((b, H_GDN, DK_GDN, DV_GDN), F32)],
        scratch_shapes=[pltpu.VMEM((H_GDN, DK_GDN, DV_GDN), F32)],
        compiler_params=_params("parallel", "arbitrary"),
        name="gdn_scan")(x, qk, u, w, qg, kg, egl, s0)


def _gdn(q, k, v, g, s0, b, t):
    c = GDN_CHUNK
    m, qk, u, w, qg, kg, egl = _gdn_chunk(q, k, v, g)
    nc = m.shape[0]
    pad = (-nc) % LANES
    mt = jnp.pad(m.reshape(nc, c * c), ((0, pad), (0, 0))).T.reshape(c, c, nc + pad)
    x = _gdn_inv(mt).reshape(c * c, nc + pad).T[:nc].reshape(nc, c, c)
    return _gdn_scan(x, qk, u, w, qg, kg, egl, s0, b, t)


def _mem_kv_kernel(m_ref, g_ref, w_ref, kg_ref, mk_ref, mv_ref):
    x = m_ref[...]
    ms = jnp.mean(x * x, axis=-1, keepdims=True)
    h = (x * lax.rsqrt(ms + EPS) * g_ref[...]).astype(BF16)
    kv = _dot(h, w_ref[...])
    mk_ref[...] = _headnorm128(kv[:, 0:MEM_W], kg_ref[...], H_MEM)
    mv_ref[...] = kv[:, MEM_W:]


def _mem_kv(mem, gain, w_kv, k_gain):
    n, d = mem.shape
    tm = 256
    return pl.pallas_call(
        _mem_kv_kernel,
        grid=(n // tm,),
        in_specs=[pl.BlockSpec((tm, d), lambda i: (i, 0)), pl.BlockSpec((1, d), lambda i: (0, 0)),
                  pl.BlockSpec((d, 2 * MEM_W), lambda i: (0, 0)), pl.BlockSpec((1, LANES), lambda i: (0, 0))],
        out_specs=[pl.BlockSpec((tm, MEM_W), lambda i: (i, 0))] * 2,
        out_shape=[jax.ShapeDtypeStruct((n, MEM_W), F32)] * 2,
        compiler_params=_params("parallel"),
        name="mem_kv")(mem, gain, w_kv, k_gain)


def _mem_attn_kernel(q_ref, mk_ref, mv_ref, g_ref, o_ref, *, cast):
    dt = BF16 if cast else F32
    qn = _headnorm128(q_ref[...], g_ref[...], H_MEM)
    outs = []
    for h in range(H_MEM):
        cols = slice(h * LANES, (h + 1) * LANES)
        s = _dot_nt(qn[:, cols].astype(dt), mk_ref[0, :, cols].astype(dt)) * (DH_MEM ** -0.5)
        p = jnp.exp(s - jnp.max(s, axis=-1, keepdims=True))
        p = p / jnp.sum(p, axis=-1, keepdims=True)
        outs.append(_dot(p.astype(dt), mv_ref[0, :, cols].astype(dt)))
    o_ref[...] = jnp.concatenate(outs, axis=1)


def _mem_attn(p, mk, mv, q_gain, b, t):
    tm = min(512, t)
    nt = t // tm
    n_mem = mk.shape[1]
    kv = pl.BlockSpec((1, n_mem, MEM_W), lambda i, j: (i, 0, 0))
    return pl.pallas_call(
        functools.partial(_mem_attn_kernel, cast=tm >= 16),
        grid=(b, nt),
        in_specs=[pl.BlockSpec((tm, MEM_W), lambda i, j: (i * nt + j, MQ)), kv, kv,
                  pl.BlockSpec((1, LANES), lambda i, j: (0, 0))],
        out_specs=pl.BlockSpec((tm, MEM_W), lambda i, j: (i * nt + j, 0)),
        out_shape=jax.ShapeDtypeStruct((b * t, MEM_W), F32),
        compiler_params=_params("parallel", "parallel"),
        name="mem_attn")(p, mk, mv, q_gain)


def _out_kernel(of_ref, og_ref, om_ref, fz_ref, gz_ref, mz_ref, g0_ref, g1_ref, g2_ref, x_ref,
                wf_ref, wg_ref, wm_ref, wo_ref, gg_ref, y_ref):
    a = (of_ref[...] * _silu(fz_ref[...])).astype(BF16)
    b = (_headnorm128(og_ref[...], gg_ref[...], H_GDN) * _silu(gz_ref[...])).astype(BF16)
    c = (om_ref[...] * _silu(mz_ref[...])).astype(BF16)
    merged = (_sigmoid(g0_ref[...]) * _dot(a, wf_ref[...]) + _sigmoid(g1_ref[...]) * _dot(b, wg_ref[...])
              + _sigmoid(g2_ref[...]) * _dot(c, wm_ref[...]))
    y_ref[...] = x_ref[...] + _dot(merged.astype(BF16), wo_ref[...])


def _out_proj(o_fox, o_gdn, o_mem, p, x, wf, wg, wm, wo, gdn_gain):
    n, d = x.shape
    tm = min(256, n)
    branch = pl.BlockSpec((tm, FOX_W), lambda i: (i, 0))
    pz = lambda blk: pl.BlockSpec((tm, FOX_W), lambda i, blk=blk: (i, blk))
    pg = lambda blk: pl.BlockSpec((tm, d), lambda i, blk=blk: (i, blk))
    const = lambda shape: pl.BlockSpec(shape, lambda i: (0, 0))
    return pl.pallas_call(
        _out_kernel,
        grid=(n // tm,),
        in_specs=[branch, branch, branch, pz(FZ), pz(GZ), pz(MZ), pg(GATE0), pg(GATE0 + 1), pg(GATE0 + 2),
                  pl.BlockSpec((tm, d), lambda i: (i, 0)),
                  const(wf.shape), const(wg.shape), const(wm.shape), const(wo.shape), const((1, LANES))],
        out_specs=pl.BlockSpec((tm, d), lambda i: (i, 0)),
        out_shape=jax.ShapeDtypeStruct((n, d), F32),
        compiler_params=_params("parallel"),
        name="out_proj")(o_fox, o_gdn, o_mem, p, p, p, p, p, p, x, wf, wg, wm, wo, gdn_gain)


def _split_w_in(w):
    o = np.cumsum([0, FOX_W, FOX_W, FOX_W, FOX_W, H_FOX, 3 * GDN_W, GDN_W, H_GDN, H_GDN, MEM_W, MEM_W])
    big = jnp.concatenate([w[:, o[0]:o[4]], w[:, o[5]:o[7]], w[:, o[9]:]], axis=1).astype(BF16)
    small = jnp.concatenate([w[:, o[4]:o[5]], w[:, o[7]:o[9]]], axis=1)
    small = jnp.pad(small, ((0, 0), (0, LANES - small.shape[1]))).astype(BF16)
    return big, small


def kernel(x_prompt, x_sample, cache_fox_k, cache_fox_v, cache_fox_logf, state_gdn, state_gdn_conv,
           cache_mem_k, cache_mem_v, page_table, mem_prompt, ln_gain, w_in, fox_q_gain, fox_k_gain,
           fox_f_bias, gdn_conv_w, gdn_A_log, gdn_dt_bias, gdn_out_gain, mem_norm_gain, w_mem_kv,
           mem_q_gain, mem_k_gain, w_fox_br, w_gdn_br, w_mem_br, w_out):
    b, t, d = x_prompt.shape
    bd_, td, _ = x_sample.shape
    depth, n_pool, page = cache_fox_k.shape[:3]
    n_pages = page_table.shape[1]
    n_mem = mem_prompt.shape[1]
    c = GDN_CHUNK
    assert page == LANES and t % c == 0 and td <= c and td % 8 == 0

    kc = cache_fox_k.reshape(depth * n_pool, page, FOX_W)
    vc = cache_fox_v.reshape(depth * n_pool, page, FOX_W)
    lft = jnp.swapaxes(cache_fox_logf, 2, 3).reshape(depth * n_pool * H_FOX, page)
    r3, s3 = [z.reshape(depth * n_pool, H_FOX, page) for z in _page_logf(lft)]

    yp = x_prompt.reshape(b * t, d)
    ys = x_sample.reshape(bd_ * td, d)
    eye = jnp.eye(H_FOX, dtype=F32)
    outs = [[] for _ in range(12)]
    for l in range(depth):
        w_big, w_small = _split_w_in(w_in[l])
        gain = ln_gain[l][None, :].astype(F32)
        wf, wg, wm, wo = [z[l].astype(BF16) for z in (w_fox_br, w_gdn_br, w_mem_br, w_out)]
        gdn_gain = gdn_out_gain[l][None, :].astype(F32)
        mq_gain = mem_q_gain[l][None, :].astype(F32)

        p, ps = _in_proj(yp, gain, w_big, w_small, 512)
        kn, fv, lf, qa, ka, va = _fox_prep_prompt(p, ps, fox_q_gain[l], fox_k_gain[l], fox_f_bias[l], b, t)
        o_fox = _fox_attn_prompt(qa, ka, va, b, t).reshape(b * t, FOX_W)
        gq, gk, gv, gg = _gdn_prep(p, ps, jnp.zeros((b, 8, 3 * GDN_W), F32), gdn_conv_w[l], gdn_A_log[l],
                                   gdn_dt_bias[l], b, t)
        o_gdn, s_new = _gdn(gq, gk, gv, gg, jnp.zeros((b, H_GDN, DK_GDN, DV_GDN), F32), b, t)
        mk, mv = _mem_kv(mem_prompt.reshape(b * n_mem, d), mem_norm_gain[l][None, :].astype(F32),
                         w_mem_kv[l].astype(BF16), mem_k_gain[l][None, :].astype(F32))
        o_mem = _mem_attn(p, mk.reshape(b, n_mem, MEM_W), mv.reshape(b, n_mem, MEM_W), mq_gain, b, t)
        conv_tail = p.reshape(b, t, -1)[:, t - (CONV_W - 1):, GQ * FOX_W:GQ * FOX_W + 3 * GDN_W]
        yp = _out_proj(o_fox, o_gdn, o_mem, p, yp, wf, wg, wm, wo, gdn_gain)
        for lst, val in zip(outs[:7], (kn.reshape(b, t, H_FOX, DH_FOX), fv.reshape(b, t, H_FOX, DH_FOX),
                                       lf[:, :, :H_FOX], s_new, conv_tail,
                                       mk.reshape(b, n_mem, H_MEM, DH_MEM), mv.reshape(b, n_mem, H_MEM, DH_MEM))):
            lst.append(val)

        p, ps = _in_proj(ys, gain, w_big, w_small, bd_ * td)
        qn, kn, lf, cn = _fox_sample_prep(p, ps, fox_q_gain[l], fox_k_gain[l], fox_f_bias[l], td)
        fv = p[:, FV * FOX_W:(FV + 1) * FOX_W]
        q4 = qn.reshape(bd_, td, H_FOX, DH_FOX).transpose(0, 2, 1, 3)
        qbd = (q4[:, :, :, None, :] * eye[None, :, None, :, None]).reshape(bd_, H_FOX * td, FOX_W).astype(BF16)
        cn3 = cn.reshape(bd_, td, LANES)[:, :, :H_FOX].transpose(0, 2, 1)
        cnq = jnp.broadcast_to(cn3.reshape(bd_, H_FOX * td, 1), (bd_, H_FOX * td, LANES))
        cnk = jnp.broadcast_to(cn3[:, :, None, :], (bd_, H_FOX, td, td)).reshape(bd_, H_FOX * td, td)
        cnk = jnp.pad(cnk, ((0, 0), (0, 0), (0, LANES - td)))
        pad_rows = lambda z: jnp.pad(z.reshape(bd_, td, FOX_W), ((0, 0), (0, page - td), (0, 0))).astype(BF16)
        pidx = (l * n_pool + page_table).reshape(-1).astype(jnp.int32)
        o_fox = _fox_decode(pidx, qbd, cnq, cnk, pad_rows(kn), pad_rows(fv), kc, vc, r3, s3, n_pages, td)
        o_fox = o_fox.reshape(bd_ * td, FOX_W)

        init = jnp.pad(state_gdn_conv[l].astype(F32), ((0, 0), (8 - (CONV_W - 1), 0), (0, 0)))
        gq, gk, gv, gg = _gdn_prep(p, ps, init, gdn_conv_w[l], gdn_A_log[l], gdn_dt_bias[l], bd_, td)
        padc = lambda z: jnp.pad(z.reshape(bd_, td, -1), ((0, 0), (0, c - td), (0, 0))).reshape(bd_ * c, -1)
        o_gdn, s_new = _gdn(padc(gq), padc(gk), padc(gv), padc(gg), state_gdn[l].astype(F32), bd_, c)
        o_gdn = o_gdn.reshape(bd_, c, GDN_W)[:, :td].reshape(bd_ * td, GDN_W)
        o_mem = _mem_attn(p, cache_mem_k[l].reshape(bd_, n_mem, MEM_W), cache_mem_v[l].reshape(bd_, n_mem, MEM_W),
                          mq_gain, bd_, td)
        conv_tail = jnp.concatenate([state_gdn_conv[l].astype(F32),
                                     p.reshape(bd_, td, -1)[:, :, GQ * FOX_W:GQ * FOX_W + 3 * GDN_W]],
                                    axis=1)[:, td:]
        ys = _out_proj(o_fox, o_gdn, o_mem, p, ys, wf, wg, wm, wo, gdn_gain)
        for lst, val in zip(outs[7:], (kn.reshape(bd_, td, H_FOX, DH_FOX), fv.reshape(bd_, td, H_FOX, DH_FOX),
                                       lf.reshape(bd_, td, LANES)[:, :, :H_FOX], s_new, conv_tail)):
            lst.append(val)

    return (yp.reshape(b, t, d), ys.reshape(bd_, td, d)) + tuple(jnp.stack(z) for z in outs)
```

```python
import functools

import numpy as np
import jax
import jax.numpy as jnp
from jax import lax
from jax.experimental import pallas as pl
from jax.experimental.pallas import tpu as pltpu

F32, BF16 = jnp.float32, jnp.bfloat16

H_FOX, DH_FOX = 8, 64
FOX_W = H_FOX * DH_FOX
H_GDN, DK_GDN, DV_GDN = 4, 128, 128
GDN_W = H_GDN * DV_GDN
CONV_W = 4
GDN_CHUNK = 64
H_MEM, DH_MEM = 4, 128
MEM_W = H_MEM * DH_MEM
N_BRANCH = 3
EPS = 1e-6
NEG = -1e30
LANES = 128
VMEM_LIMIT = 56 * 1024 * 1024
DECODE_PAGES_PER_STEP = 16

FQ, FK, FV, FZ, GQ, GK, GV, GZ, MQ, MZ = range(10)
GATE0 = 5
SM_FF, SM_GA, SM_GB = 0, 8, 12


def _params(*sem):
    return pltpu.CompilerParams(dimension_semantics=sem, vmem_limit_bytes=VMEM_LIMIT)


def _dot(a, b):
    return jnp.dot(a, b, preferred_element_type=F32)


def _dot_nt(a, b):
    return lax.dot_general(a, b, (((1,), (1,)), ((), ())), preferred_element_type=F32)


def _dot_tn(a, b):
    return lax.dot_general(a, b, (((0,), (0,)), ((), ())), preferred_element_type=F32)


def _split2(x):
    hi = x.astype(BF16)
    lo = (x - hi.astype(F32)).astype(BF16)
    return hi, lo


def _split3(x):
    hi = x.astype(BF16)
    r = x - hi.astype(F32)
    mid = r.astype(BF16)
    lo = (r - mid.astype(F32)).astype(BF16)
    return hi, mid, lo


def _dot3_lhs_exact(m, x):
    hi, mid, lo = _split3(x)
    return _dot(m, hi) + _dot(m, mid) + _dot(m, lo)


def _dot3_rhs_exact(x, m):
    hi, mid, lo = _split3(x)
    return _dot(hi, m) + _dot(mid, m) + _dot(lo, m)


def _sigmoid(x):
    return 1.0 / (1.0 + jnp.exp(-x))


def _silu(x):
    return x * _sigmoid(x)


def _softplus(x):
    return jnp.maximum(x, 0.0) + jnp.log1p(jnp.exp(-jnp.abs(x)))


def _log_sigmoid(x):
    return jnp.minimum(x, 0.0) - jnp.log1p(jnp.exp(-jnp.abs(x)))


def _lane_bcast(x, lane):
    return jnp.broadcast_to(x[:, lane:lane + 1], x.shape)


def _headnorm128(x, gain, nheads):
    outs = []
    for h in range(nheads):
        seg = x[:, h * LANES:(h + 1) * LANES]
        ms = jnp.mean(seg * seg, axis=-1, keepdims=True)
        outs.append(seg * lax.rsqrt(ms + EPS) * gain)
    return jnp.concatenate(outs, axis=1)


def _in_proj_kernel(x_ref, g_ref, w_ref, ws_ref, p_ref, ps_ref, h_ref):
    @pl.when(pl.program_id(1) == 0)
    def _():
        x = x_ref[...]
        ms = jnp.mean(x * x, axis=-1, keepdims=True)
        h = (x * lax.rsqrt(ms + EPS) * g_ref[...]).astype(BF16)
        h_ref[...] = h
        ps_ref[...] = _dot(h, ws_ref[...])

    p_ref[...] = _dot(h_ref[...], w_ref[...])


def _in_proj(x, gain, w_big, w_small, tm):
    n, d = x.shape
    nb = w_big.shape[1]
    tn = 1024
    return pl.pallas_call(
        _in_proj_kernel,
        grid=(n // tm, nb // tn),
        in_specs=[pl.BlockSpec((tm, d), lambda i, j: (i, 0)),
                  pl.BlockSpec((1, d), lambda i, j: (0, 0)),
                  pl.BlockSpec((d, tn), lambda i, j: (0, j)),
                  pl.BlockSpec((d, LANES), lambda i, j: (0, 0))],
        out_specs=[pl.BlockSpec((tm, tn), lambda i, j: (i, j)),
                   pl.BlockSpec((tm, LANES), lambda i, j: (i, 0))],
        out_shape=[jax.ShapeDtypeStruct((n, nb), F32), jax.ShapeDtypeStruct((n, LANES), F32)],
        scratch_shapes=[pltpu.VMEM((tm, d), BF16)],
        compiler_params=_params("parallel", "arbitrary"),
        name="in_proj")(x, gain, w_big, w_small)


def _fox_headnorm(x, gain, bd):
    hi, lo = _split2(x * x)
    ss = _dot(hi, bd) + _dot(lo, bd)
    return x * lax.rsqrt(ss * (1.0 / DH_FOX) + EPS) * gain


def _fox_prep_kernel(fq_ref, fk_ref, fv_ref, ps_ref, qg_ref, kg_ref, fb_ref, bd_ref, tri_ref,
                     selq_ref, selk_ref, selv_ref, qrow_ref, krow_ref,
                     kn_ref, v_ref, lf_ref, qa_ref, ka_ref, va_ref, carry_ref):
    @pl.when(pl.program_id(1) == 0)
    def _():
        carry_ref[...] = jnp.zeros_like(carry_ref)

    bd = bd_ref[...]
    qn = _fox_headnorm(fq_ref[...], qg_ref[...], bd) * (DH_FOX ** -0.5)
    kn = _fox_headnorm(fk_ref[...], kg_ref[...], bd)
    fv = fv_ref[...]
    kn_ref[0] = kn
    v_ref[0] = fv
    lf = _log_sigmoid(ps_ref[...] + fb_ref[...])
    lf_ref[0] = lf
    c = _dot3_lhs_exact(tri_ref[...], lf) + carry_ref[0:1, :]
    tm = c.shape[0]
    carry_ref[...] = jnp.broadcast_to(c[tm - 1:tm, :], carry_ref.shape)
    chi, cmid, clo = _split3(c)
    qa = _dot(jnp.concatenate([qn.astype(BF16), chi, cmid, clo], axis=1), selq_ref[...]) + qrow_ref[...]
    ka = _dot(jnp.concatenate([kn.astype(BF16), chi, cmid, clo], axis=1), selk_ref[...]) + krow_ref[...]
    qa_ref[0] = qa.astype(BF16)
    ka_ref[0] = ka.astype(BF16)
    va_ref[0] = _dot(fv.astype(BF16), selv_ref[...]).astype(BF16)


def _fox_consts(tm):
    a = np.arange(FOX_W)
    bd = (a[:, None] // DH_FOX == a[None, :] // DH_FOX).astype(np.float32)
    r = np.arange(tm)
    tri = (r[None, :] <= r[:, None]).astype(np.float32)
    aw = H_FOX * LANES
    selq = np.zeros((FOX_W + 3 * LANES, aw), np.float32)
    selk = np.zeros((FOX_W + 3 * LANES, aw), np.float32)
    selv = np.zeros((FOX_W, aw), np.float32)
    qrow = np.zeros((1, aw), np.float32)
    krow = np.zeros((1, aw), np.float32)
    for h in range(H_FOX):
        for d in range(DH_FOX):
            selq[h * DH_FOX + d, h * LANES + d] = 1.0
            selk[h * DH_FOX + d, h * LANES + d] = 1.0
            selv[h * DH_FOX + d, h * LANES + (h % 2) * DH_FOX + d] = 1.0
        for e in range(3):
            selq[FOX_W + e * LANES + h, h * LANES + DH_FOX + e] = 1.0
            selk[FOX_W + e * LANES + h, h * LANES + DH_FOX + 3 + e] = -1.0
            qrow[0, h * LANES + DH_FOX + 3 + e] = 1.0
            krow[0, h * LANES + DH_FOX + e] = 1.0
    bf = lambda z: jnp.asarray(z, BF16)
    return bf(bd), bf(tri), bf(selq), bf(selk), bf(selv), jnp.asarray(qrow), jnp.asarray(krow)


def _tile_gain(g, reps):
    return jnp.tile(g.astype(F32), reps)[None, :]


def _pad_row(vals, offset):
    row = jnp.zeros((LANES,), F32)
    return lax.dynamic_update_slice(row, vals.astype(F32), (offset,))[None, :]


def _fox_prep_prompt(p, ps, qgain, kgain, fbias, b, t):
    tm = min(512, t)
    nt = t // tm
    bd, tri, selq, selk, selv, qrow, krow = _fox_consts(tm)
    aw = H_FOX * LANES
    row = lambda blk: pl.BlockSpec((tm, FOX_W), lambda i, j, blk=blk: (i * nt + j, blk))
    const = lambda shape: pl.BlockSpec(shape, lambda i, j: (0,) * len(shape))
    tok = lambda w: pl.BlockSpec((1, tm, w), lambda i, j: (i, j, 0))
    return pl.pallas_call(
        _fox_prep_kernel,
        grid=(b, nt),
        in_specs=[row(FQ), row(FK), row(FV),
                  pl.BlockSpec((tm, LANES), lambda i, j: (i * nt + j, 0)),
                  const((1, FOX_W)), const((1, FOX_W)), const((1, LANES)),
                  const(bd.shape), const(tri.shape), const(selq.shape), const(selk.shape), const(selv.shape),
                  const(qrow.shape), const(krow.shape)],
        out_specs=[tok(FOX_W), tok(FOX_W), tok(LANES), tok(aw), tok(aw), tok(aw)],
        out_shape=[jax.ShapeDtypeStruct((b, t, FOX_W), F32), jax.ShapeDtypeStruct((b, t, FOX_W), F32),
                   jax.ShapeDtypeStruct((b, t, LANES), F32),
                   jax.ShapeDtypeStruct((b, t, aw), BF16), jax.ShapeDtypeStruct((b, t, aw), BF16),
                   jax.ShapeDtypeStruct((b, t, aw), BF16)],
        scratch_shapes=[pltpu.VMEM((8, LANES), F32)],
        compiler_params=_params("parallel", "arbitrary"),
        name="fox_prep")(p, p, p, ps, _tile_gain(qgain, H_FOX), _tile_gain(kgain, H_FOX), _pad_row(fbias, SM_FF),
                         bd, tri, selq, selk, selv, qrow, krow)


def _fox_attn_kernel(q_ref, k_ref, v_ref, o_ref, m_ref, l_ref, acc_ref, *, tk):
    qi = pl.program_id(2)
    tq = q_ref.shape[1]
    row = lax.broadcasted_iota(jnp.int32, (tq, tk), 0)
    col = lax.broadcasted_iota(jnp.int32, (tq, tk), 1)
    m_ref[...] = jnp.full(m_ref.shape, NEG, F32)
    l_ref[...] = jnp.zeros_like(l_ref)
    acc_ref[...] = jnp.zeros_like(acc_ref)

    def step(ki, masked):
        r0 = pl.multiple_of(ki * tk, tk)
        for hh in range(2):
            lanes = slice(hh * LANES, (hh + 1) * LANES)
            s = _dot_nt(q_ref[0, :, lanes], k_ref[0, pl.ds(r0, tk), lanes])
            if masked:
                s = jnp.where(col <= row, s, NEG)
            m_prev = m_ref[hh]
            m_new = jnp.maximum(m_prev, jnp.max(s, axis=-1, keepdims=True))
            alpha = jnp.exp(m_prev - m_new)
            p = jnp.exp(s - m_new)
            l_ref[hh] = alpha * l_ref[hh] + jnp.sum(p, axis=-1, keepdims=True)
            acc_ref[hh] = alpha * acc_ref[hh] + _dot(p.astype(BF16), v_ref[0, pl.ds(r0, tk), lanes])
            m_ref[hh] = m_new

    def body(ki, carry):
        step(ki, False)
        return carry

    lax.fori_loop(0, qi, body, 0)
    step(qi, True)
    o_ref[0] = acc_ref[0] / l_ref[0] + acc_ref[1] / l_ref[1]


def _fox_attn_prompt(qa, ka, va, b, t):
    tq = min(512, t)
    pair = 2 * LANES
    return pl.pallas_call(
        functools.partial(_fox_attn_kernel, tk=tq),
        grid=(b, H_FOX // 2, t // tq),
        in_specs=[pl.BlockSpec((1, tq, pair), lambda i, h, q: (i, q, h)),
                  pl.BlockSpec((1, t, pair), lambda i, h, q: (i, 0, h)),
                  pl.BlockSpec((1, t, pair), lambda i, h, q: (i, 0, h))],
        out_specs=pl.BlockSpec((1, tq, LANES), lambda i, h, q: (i, q, h)),
        out_shape=jax.ShapeDtypeStruct((b, t, FOX_W), F32),
        scratch_shapes=[pltpu.VMEM((2, tq, 1), F32), pltpu.VMEM((2, tq, 1), F32), pltpu.VMEM((2, tq, LANES), F32)],
        compiler_params=_params("parallel", "parallel", "arbitrary"),
        name="fox_attn")(qa, ka, va)


def _fox_sample_prep_kernel(fq_ref, fk_ref, ps_ref, qg_ref, kg_ref, fb_ref, bd_ref, tri_ref,
                            qn_ref, kn_ref, lf_ref, cn_ref):
    bd = bd_ref[...]
    qn_ref[...] = _fox_headnorm(fq_ref[...], qg_ref[...], bd) * (DH_FOX ** -0.5)
    kn_ref[...] = _fox_headnorm(fk_ref[...], kg_ref[...], bd)
    lf = _log_sigmoid(ps_ref[...] + fb_ref[...])
    lf_ref[...] = lf
    cn_ref[...] = _dot3_lhs_exact(tri_ref[...], lf)


def _fox_sample_prep(p, ps, qgain, kgain, fbias, td):
    n = p.shape[0]
    a = np.arange(FOX_W)
    bd = jnp.asarray((a[:, None] // DH_FOX == a[None, :] // DH_FOX), BF16)
    r = np.arange(n)
    tri = jnp.asarray((r[None, :] <= r[:, None]) & (r[None, :] // td == r[:, None] // td), BF16)
    full = lambda shape: pl.BlockSpec(shape, lambda i: (0,) * len(shape))
    return pl.pallas_call(
        _fox_sample_prep_kernel,
        grid=(1,),
        in_specs=[pl.BlockSpec((n, FOX_W), lambda i: (0, FQ)), pl.BlockSpec((n, FOX_W), lambda i: (0, FK)),
                  full((n, LANES)), full((1, FOX_W)), full((1, FOX_W)), full((1, LANES)),
                  full(bd.shape), full(tri.shape)],
        out_specs=[full((n, FOX_W)), full((n, FOX_W)), full((n, LANES)), full((n, LANES))],
        out_shape=[jax.ShapeDtypeStruct((n, FOX_W), F32), jax.ShapeDtypeStruct((n, FOX_W), F32),
                   jax.ShapeDtypeStruct((n, LANES), F32), jax.ShapeDtypeStruct((n, LANES), F32)],
        compiler_params=_params("arbitrary"),
        name="fox_sample_prep")(p, p, ps, _tile_gain(qgain, H_FOX), _tile_gain(kgain, H_FOX),
                                _pad_row(fbias, SM_FF), bd, tri)


def _page_logf_kernel(x_ref, u_ref, ones_ref, r_ref, s_ref):
    x = x_ref[...]
    r_ref[...] = _dot3_rhs_exact(x, u_ref[...])
    s_ref[...] = _dot3_rhs_exact(x, ones_ref[...])


def _page_logf(xt):
    n, page = xt.shape
    tm = 2048
    while n % tm:
        tm //= 2
    j = np.arange(page)
    u = jnp.asarray(j[:, None] > j[None, :], BF16)
    ones = jnp.ones((page, page), BF16)
    return pl.pallas_call(
        _page_logf_kernel,
        grid=(n // tm,),
        in_specs=[pl.BlockSpec((tm, page), lambda i: (i, 0)),
                  pl.BlockSpec((page, page), lambda i: (0, 0)), pl.BlockSpec((page, page), lambda i: (0, 0))],
        out_specs=[pl.BlockSpec((tm, page), lambda i: (i, 0)), pl.BlockSpec((tm, page), lambda i: (i, 0))],
        out_shape=[jax.ShapeDtypeStruct((n, page), F32), jax.ShapeDtypeStruct((n, page), F32)],
        compiler_params=_params("parallel"),
        name="page_logf")(xt, u, ones)


def _fox_decode_kernel(pidx_ref, q_ref, cn_ref, cnk_ref, knew_ref, vnew_ref, *rest, g, td):
    k_refs, v_refs, r_refs, s_refs = rest[0:g], rest[g:2 * g], rest[2 * g:3 * g], rest[3 * g:4 * g]
    o_ref, m_ref, l_ref, acc_ref, suf_ref = rest[4 * g:]
    p = pl.program_id(1)
    q = q_ref[0]
    cn = cn_ref[0]
    nrow = q.shape[0]

    @pl.when(p == 0)
    def _():
        s = _dot(q, knew_ref[0])
        qpos = lax.broadcasted_iota(jnp.int32, s.shape, 0) % td
        kpos = lax.broadcasted_iota(jnp.int32, s.shape, 1)
        s = jnp.where(kpos <= qpos, s + cn - cnk_ref[0], NEG)
        m = jnp.max(s, axis=-1, keepdims=True)
        pr = jnp.exp(s - m)
        m_ref[...] = m
        l_ref[...] = jnp.sum(pr, axis=-1, keepdims=True)
        acc_ref[...] = _dot_nt(pr.astype(BF16), vnew_ref[0])
        suf_ref[...] = jnp.zeros_like(suf_ref)

    suf = suf_ref[...]
    scores = []
    for j in range(g):
        z = r_refs[j][0] + suf
        bias = jnp.concatenate([jnp.broadcast_to(z[h:h + 1, :], (td, LANES)) for h in range(H_FOX)], axis=0)
        scores.append(_dot(q, k_refs[j][0].astype(BF16)) + bias + cn)
        suf = suf + s_refs[j][0]
    suf_ref[...] = suf
    m_prev = m_ref[...]
    m_new = m_prev
    for s in scores:
        m_new = jnp.maximum(m_new, jnp.max(s, axis=-1, keepdims=True))
    alpha = jnp.exp(m_prev - m_new)
    l = alpha * l_ref[...]
    acc = alpha * acc_ref[...]
    for j in range(g):
        pr = jnp.exp(scores[j] - m_new)
        l = l + jnp.sum(pr, axis=-1, keepdims=True)
        acc = acc + _dot_nt(pr.astype(BF16), v_refs[j][0].astype(BF16))
    m_ref[...] = m_new
    l_ref[...] = l
    acc_ref[...] = acc

    @pl.when(p == pl.num_programs(1) - 1)
    def _():
        o = acc / l
        head = lax.broadcasted_iota(jnp.int32, (td, FOX_W), 1) // DH_FOX
        out = jnp.zeros((td, FOX_W), F32)
        for h in range(H_FOX):
            out = out + jnp.where(head == h, o[h * td:(h + 1) * td, :], 0.0)
        o_ref[0] = out


def _fox_decode(pidx, qbd, cn, cnk, knew, vnew, kc, vc, r3, s3, n_pages, td):
    bd_, nrow, _ = qbd.shape
    page = kc.shape[2]
    g = min(DECODE_PAGES_PER_STEP, n_pages)
    steps = n_pages // g

    def page_map(j):
        return lambda b, p, idx: (idx[b * n_pages + (n_pages - 1 - (p * g + j))], 0, 0)

    per_b = lambda shape: pl.BlockSpec((1,) + shape, lambda b, p, idx: (b, 0, 0))
    in_specs = [per_b((nrow, FOX_W)), per_b((nrow, LANES)), per_b((nrow, LANES)),
                per_b((FOX_W, page)), per_b((FOX_W, page))]
    in_specs += [pl.BlockSpec((1, FOX_W, page), page_map(j)) for j in range(g)]
    in_specs += [pl.BlockSpec((1, FOX_W, page), page_map(j)) for j in range(g)]
    in_specs += [pl.BlockSpec((1, H_FOX, LANES), page_map(j)) for j in range(g)]
    in_specs += [pl.BlockSpec((1, H_FOX, LANES), page_map(j)) for j in range(g)]
    grid_spec = pltpu.PrefetchScalarGridSpec(
        num_scalar_prefetch=1, grid=(bd_, steps), in_specs=in_specs,
        out_specs=pl.BlockSpec((1, td, FOX_W), lambda b, p, idx: (b, 0, 0)),
        scratch_shapes=[pltpu.VMEM((nrow, 1), F32), pltpu.VMEM((nrow, 1), F32), pltpu.VMEM((nrow, FOX_W), F32),
                        pltpu.VMEM((H_FOX, LANES), F32)])
    return pl.pallas_call(
        functools.partial(_fox_decode_kernel, g=g, td=td),
        grid_spec=grid_spec,
        out_shape=jax.ShapeDtypeStruct((bd_, td, FOX_W), F32),
        compiler_params=_params("parallel", "arbitrary"),
        name="fox_decode")(pidx, qbd, cn, cnk, knew, vnew, *([kc] * g), *([vc] * g), *([r3] * g), *([s3] * g))


def _gdn_prep_kernel(gq_ref, gk_ref, gv_ref, ps_ref, init_ref, w_ref, alog_ref, dtb_ref,
                     q_ref, k_ref, v_ref, g_ref, ext_ref):
    tm = gq_ref.shape[0]

    @pl.when(pl.program_id(1) == 0)
    def _():
        ext_ref[0:8, :] = init_ref[0]

    @pl.when(pl.program_id(1) > 0)
    def _():
        ext_ref[0:8, :] = ext_ref[tm:tm + 8, :]

    ext_ref[8:8 + tm, 0:GDN_W] = gq_ref[...]
    ext_ref[8:8 + tm, GDN_W:2 * GDN_W] = gk_ref[...]
    ext_ref[8:8 + tm, 2 * GDN_W:3 * GDN_W] = gv_ref[...]
    conv = ext_ref[8:8 + tm, :] * w_ref[CONV_W - 1:CONV_W, :]
    for i in range(CONV_W - 1):
        off = 8 - (CONV_W - 1) + i
        conv = conv + ext_ref[off:off + tm, :] * w_ref[i:i + 1, :]
    s = _silu(conv)

    def l2(x):
        outs = []
        for h in range(H_GDN):
            seg = x[:, h * LANES:(h + 1) * LANES]
            outs.append(seg * lax.rsqrt(jnp.sum(seg * seg, axis=-1, keepdims=True) + EPS))
        return jnp.concatenate(outs, axis=1)

    q_ref[...] = l2(s[:, 0:GDN_W]) * (DK_GDN ** -0.5)
    k_ref[...] = l2(s[:, GDN_W:2 * GDN_W])
    v_ref[...] = s[:, 2 * GDN_W:3 * GDN_W]
    ps = ps_ref[...]
    lane = lax.broadcasted_iota(jnp.int32, ps.shape, 1)
    gate = -jnp.exp(alog_ref[...]) * _softplus(ps + dtb_ref[...])
    beta = _sigmoid(ps)
    g_ref[...] = jnp.where((lane >= SM_GA) & (lane < SM_GB), gate,
                           jnp.where((lane >= SM_GB) & (lane < SM_GB + H_GDN), beta, 0.0))


def _gdn_prep(p, ps, init, conv_w, a_log, dt_bias, b, t):
    tm = min(512, t)
    nt = t // tm
    n = b * t
    row = lambda blk: pl.BlockSpec((tm, GDN_W), lambda i, j, blk=blk: (i * nt + j, blk))
    out = pl.BlockSpec((tm, GDN_W), lambda i, j: (i * nt + j, 0))
    small = pl.BlockSpec((tm, LANES), lambda i, j: (i * nt + j, 0))
    const = lambda shape: pl.BlockSpec(shape, lambda i, j: (0,) * len(shape))
    return pl.pallas_call(
        _gdn_prep_kernel,
        grid=(b, nt),
        in_specs=[row(GQ), row(GK), row(GV), small,
                  pl.BlockSpec((1, 8, 3 * GDN_W), lambda i, j: (i, 0, 0)),
                  const((CONV_W, 3 * GDN_W)), const((1, LANES)), const((1, LANES))],
        out_specs=[out, out, out, small],
        out_shape=[jax.ShapeDtypeStruct((n, GDN_W), F32)] * 3 + [jax.ShapeDtypeStruct((n, LANES), F32)],
        scratch_shapes=[pltpu.VMEM((tm + 8, 3 * GDN_W), F32)],
        compiler_params=_params("parallel", "arbitrary"),
        name="gdn_prep")(p, p, p, ps, init, conv_w.astype(F32), _pad_row(a_log, SM_GA), _pad_row(dt_bias, SM_GA))


def _gdn_chunk_kernel(q_ref, k_ref, v_ref, g_ref, tri_ref,
                      m_ref, qk_ref, u_ref, w_ref, qg_ref, kg_ref, egl_ref, *, nch):
    c = GDN_CHUNK
    tri = tri_ref[...]
    ri = lax.broadcasted_iota(jnp.int32, (c, c), 0)
    ci = lax.broadcasted_iota(jnp.int32, (c, c), 1)
    lane = lax.broadcasted_iota(jnp.int32, (c, LANES), 1)
    for cc in range(nch):
        rows = slice(cc * c, (cc + 1) * c)
        gt = g_ref[rows, :]
        gcum = _dot3_lhs_exact(tri, gt)
        for h in range(H_GDN):
            cols = slice(h * LANES, (h + 1) * LANES)
            gcol = _lane_bcast(gcum, SM_GA + h)
            beta = _lane_bcast(gt, SM_GB + h)
            hi, mid, lo = [z.astype(F32) for z in _split3(gcol)]
            a = jnp.where(lane == 0, hi, jnp.where(lane == 1, mid, jnp.where(lane == 2, lo,
                          jnp.where(lane < 6, 1.0, 0.0))))
            bm = jnp.where(lane < 3, 1.0, jnp.where(lane == 3, -hi, jnp.where(lane == 4, -mid,
                           jnp.where(lane == 5, -lo, 0.0))))
            diff = _dot_nt(a.astype(BF16), bm.astype(BF16))
            decay = jnp.exp(jnp.where(ci <= ri, diff, NEG))
            qh, kh, vh = q_ref[rows, cols], k_ref[rows, cols], v_ref[rows, cols]
            kb = kh * beta
            khb = kh.astype(BF16)
            m_ref[cc * H_GDN + h] = jnp.where(ci < ri, _dot_nt(kb.astype(BF16), khb) * decay, 0.0)
            qk_ref[cc * H_GDN + h] = _dot_nt(qh.astype(BF16), khb) * decay
            eg = jnp.exp(gcol)
            glast = gcol[c - 1:c, :]
            u_ref[rows, cols] = vh * beta
            w_ref[rows, cols] = kb * eg
            qg_ref[rows, cols] = qh * eg
            kg_ref[rows, cols] = kh * jnp.exp(glast - gcol)
            egl_ref[cc * H_GDN + h] = jnp.broadcast_to(jnp.exp(glast), (8, LANES))


def _gdn_chunk(q, k, v, g):
    n = q.shape[0]
    c = GDN_CHUNK
    tm = min(256, n)
    nch = tm // c
    r = np.arange(c)
    tri = jnp.asarray(r[None, :] <= r[:, None], BF16)
    nc = n // c * H_GDN
    row = pl.BlockSpec((tm, GDN_W), lambda i: (i, 0))
    blk = lambda shape: pl.BlockSpec((nch * H_GDN,) + shape, lambda i: (i, 0, 0))
    return pl.pallas_call(
        functools.partial(_gdn_chunk_kernel, nch=nch),
        grid=(n // tm,),
        in_specs=[row, row, row, pl.BlockSpec((tm, LANES), lambda i: (i, 0)), pl.BlockSpec((c, c), lambda i: (0, 0))],
        out_specs=[blk((c, c)), blk((c, c)), row, row, row, row, blk((8, LANES))],
        out_shape=[jax.ShapeDtypeStruct((nc, c, c), F32), jax.ShapeDtypeStruct((nc, c, c), F32)]
                  + [jax.ShapeDtypeStruct((n, GDN_W), F32)] * 4 + [jax.ShapeDtypeStruct((nc, 8, LANES), F32)],
        compiler_params=_params("parallel"),
        name="gdn_chunk")(q, k, v, g, tri)


def _gdn_inv_kernel(m_ref, x_ref):
    c = GDN_CHUNK
    for i in range(c):
        ext = 8 * (i // 8 + 1)
        e_i = (lax.broadcasted_iota(jnp.int32, (ext, LANES), 0) == i).astype(F32)

        def body(j, acc, i=i, ext=ext):
            return acc - m_ref[i, pl.ds(j, 1), :] * x_ref[j, 0:ext, :]

        x_ref[i, 0:ext, :] = lax.fori_loop(0, i, body, e_i)
        if ext < c:
            x_ref[i, ext:c, :] = jnp.zeros((c - ext, LANES), F32)


def _gdn_inv(mt):
    c = GDN_CHUNK
    nc = mt.shape[2]
    spec = pl.BlockSpec((c, c, LANES), lambda i: (0, 0, i))
    return pl.pallas_call(
        _gdn_inv_kernel,
        grid=(nc // LANES,),
        in_specs=[spec], out_specs=spec,
        out_shape=jax.ShapeDtypeStruct((c, c, nc), F32),
        compiler_params=_params("parallel"),
        name="gdn_inv")(mt)


def _gdn_scan_kernel(x_ref, qk_ref, u_ref, w_ref, qg_ref, kg_ref, egl_ref, s0_ref, o_ref, sout_ref,
                     s_ref, us_ref, wq_ref, *, nch):
    c = GDN_CHUNK

    @pl.when(pl.program_id(1) == 0)
    def _():
        s_ref[...] = s0_ref[0]

    def solve(cc, carry):
        r0 = pl.multiple_of(cc * c, c)
        r2 = pl.multiple_of(cc * 2 * c, 2 * c)
        for h in range(H_GDN):
            cols = slice(h * LANES, (h + 1) * LANES)
            rhs = jnp.concatenate([u_ref[pl.ds(r0, c), cols], w_ref[pl.ds(r0, c), cols]], axis=1)
            xh, xl = _split2(x_ref[cc * H_GDN + h])
            rh, rl = _split2(rhs)
            sol = _dot(xh, rh) + _dot(xh, rl) + _dot(xl, rh)
            us_ref[pl.ds(r0, c), cols] = sol[:, 0:DV_GDN]
            wq_ref[pl.ds(r2, c), cols] = sol[:, DV_GDN:].astype(BF16)
            wq_ref[pl.ds(r2 + c, c), cols] = qg_ref[pl.ds(r0, c), cols].astype(BF16)
        return carry

    lax.fori_loop(0, nch, solve, 0)

    def body(cc, carry):
        r0 = pl.multiple_of(cc * c, c)
        r2 = pl.multiple_of(cc * 2 * c, 2 * c)
        for h in range(H_GDN):
            cols = slice(h * LANES, (h + 1) * LANES)
            st = s_ref[h]
            both = _dot(wq_ref[pl.ds(r2, 2 * c), cols], st.astype(BF16))
            db = (us_ref[pl.ds(r0, c), cols] - both[0:c]).astype(BF16)
            o_ref[pl.ds(r0, c), cols] = both[c:] + _dot(qk_ref[cc * H_GDN + h].astype(BF16), db)
            egl = jnp.broadcast_to(egl_ref[cc * H_GDN + h][0:1, :], (DK_GDN, DV_GDN))
            s_ref[h] = st * egl + _dot_tn(kg_ref[pl.ds(r0, c), cols].astype(BF16), db)
        return carry

    lax.fori_loop(0, nch, body, 0)

    @pl.when(pl.program_id(1) == pl.num_programs(1) - 1)
    def _():
        sout_ref[0] = s_ref[...]


def _gdn_scan(x, qk, u, w, qg, kg, egl, s0, b, t):
    c = GDN_CHUNK
    tg = min(1024, t)
    ng = t // tg
    nch = tg // c
    row = pl.BlockSpec((tg, GDN_W), lambda i, j: (i * ng + j, 0))
    blk = lambda shape: pl.BlockSpec((nch * H_GDN,) + shape, lambda i, j: (i * ng + j, 0, 0))
    st = pl.BlockSpec((1, H_GDN, DK_GDN, DV_GDN), lambda i, j: (i, 0, 0, 0))
    return pl.pallas_call(
        functools.partial(_gdn_scan_kernel, nch=nch),
        grid=(b, ng),
        in_specs=[blk((c, c)), blk((c, c)), row, row, row, row, blk((8, LANES)), st],
        out_specs=[row, st],
        out_shape=[jax.ShapeDtypeStruct((b * t, GDN_W), F32), jax.ShapeDtypeStruct((b, H_GDN, DK_GDN, DV_GDN), F32)],
        scratch_shapes=[pltpu.VMEM((H_GDN, DK_GDN, DV_GDN), F32), pltpu.VMEM((tg, GDN_W), F32),
                        pltpu.VMEM((2 * tg, GDN_W), BF16)],
        compiler_params=_params("parallel", "arbitrary"),
        name="gdn_scan")(x, qk, u, w, qg, kg, egl, s0)


def _gdn(q, k, v, g, s0, b, t):
    c = GDN_CHUNK
    m, qk, u, w, qg, kg, egl = _gdn_chunk(q, k, v, g)
    nc = m.shape[0]
    pad = (-nc) % LANES
    mt = jnp.pad(m.reshape(nc, c * c), ((0, pad), (0, 0))).T.reshape(c, c, nc + pad)
    x = _gdn_inv(mt).reshape(c * c, nc + pad).T[:nc].reshape(nc, c, c)
    return _gdn_scan(x, qk, u, w, qg, kg, egl, s0, b, t)


def _mem_kv_kernel(m_ref, g_ref, w_ref, kg_ref, mk_ref, mv_ref):
    x = m_ref[...]
    ms = jnp.mean(x * x, axis=-1, keepdims=True)
    h = (x * lax.rsqrt(ms + EPS) * g_ref[...]).astype(BF16)
    kv = _dot(h, w_ref[...])
    mk_ref[...] = _headnorm128(kv[:, 0:MEM_W], kg_ref[...], H_MEM)
    mv_ref[...] = kv[:, MEM_W:]


def _mem_kv(mem, gain, w_kv, k_gain):
    n, d = mem.shape
    tm = 256
    return pl.pallas_call(
        _mem_kv_kernel,
        grid=(n // tm,),
        in_specs=[pl.BlockSpec((tm, d), lambda i: (i, 0)), pl.BlockSpec((1, d), lambda i: (0, 0)),
                  pl.BlockSpec((d, 2 * MEM_W), lambda i: (0, 0)), pl.BlockSpec((1, LANES), lambda i: (0, 0))],
        out_specs=[pl.BlockSpec((tm, MEM_W), lambda i: (i, 0))] * 2,
        out_shape=[jax.ShapeDtypeStruct((n, MEM_W), F32)] * 2,
        compiler_params=_params("parallel"),
        name="mem_kv")(mem, gain, w_kv, k_gain)


def _mem_attn_kernel(q_ref, mk_ref, mv_ref, g_ref, o_ref, *, cast):
    dt = BF16 if cast else F32
    qn = _headnorm128(q_ref[...], g_ref[...], H_MEM)
    outs = []
    for h in range(H_MEM):
        cols = slice(h * LANES, (h + 1) * LANES)
        s = _dot_nt(qn[:, cols].astype(dt), mk_ref[0, :, cols].astype(dt)) * (DH_MEM ** -0.5)
        p = jnp.exp(s - jnp.max(s, axis=-1, keepdims=True))
        p = p / jnp.sum(p, axis=-1, keepdims=True)
        outs.append(_dot(p.astype(dt), mv_ref[0, :, cols].astype(dt)))
    o_ref[...] = jnp.concatenate(outs, axis=1)


def _mem_attn(p, mk, mv, q_gain, b, t):
    tm = min(512, t)
    nt = t // tm
    n_mem = mk.shape[1]
    kv = pl.BlockSpec((1, n_mem, MEM_W), lambda i, j: (i, 0, 0))
    return pl.pallas_call(
        functools.partial(_mem_attn_kernel, cast=tm >= 16),
        grid=(b, nt),
        in_specs=[pl.BlockSpec((tm, MEM_W), lambda i, j: (i * nt + j, MQ)), kv, kv,
                  pl.BlockSpec((1, LANES), lambda i, j: (0, 0))],
        out_specs=pl.BlockSpec((tm, MEM_W), lambda i, j: (i * nt + j, 0)),
        out_shape=jax.ShapeDtypeStruct((b * t, MEM_W), F32),
        compiler_params=_params("parallel", "parallel"),
        name="mem_attn")(p, mk, mv, q_gain)


def _out_kernel(of_ref, og_ref, om_ref, fz_ref, gz_ref, mz_ref, g0_ref, g1_ref, g2_ref, x_ref,
                wf_ref, wg_ref, wm_ref, wo_ref, gg_ref, y_ref):
    a = (of_ref[...] * _silu(fz_ref[...])).astype(BF16)
    b = (_headnorm128(og_ref[...], gg_ref[...], H_GDN) * _silu(gz_ref[...])).astype(BF16)
    c = (om_ref[...] * _silu(mz_ref[...])).astype(BF16)
    merged = (_sigmoid(g0_ref[...]) * _dot(a, wf_ref[...]) + _sigmoid(g1_ref[...]) * _dot(b, wg_ref[...])
              + _sigmoid(g2_ref[...]) * _dot(c, wm_ref[...]))
    y_ref[...] = x_ref[...] + _dot(merged.astype(BF16), wo_ref[...])


def _out_proj(o_fox, o_gdn, o_mem, p, x, wf, wg, wm, wo, gdn_gain):
    n, d = x.shape
    tm = min(256, n)
    branch = pl.BlockSpec((tm, FOX_W), lambda i: (i, 0))
    pz = lambda blk: pl.BlockSpec((tm, FOX_W), lambda i, blk=blk: (i, blk))
    pg = lambda blk: pl.BlockSpec((tm, d), lambda i, blk=blk: (i, blk))
    const = lambda shape: pl.BlockSpec(shape, lambda i: (0, 0))
    return pl.pallas_call(
        _out_kernel,
        grid=(n // tm,),
        in_specs=[branch, branch, branch, pz(FZ), pz(GZ), pz(MZ), pg(GATE0), pg(GATE0 + 1), pg(GATE0 + 2),
                  pl.BlockSpec((tm, d), lambda i: (i, 0)),
                  const(wf.shape), const(wg.shape), const(wm.shape), const(wo.shape), const((1, LANES))],
        out_specs=pl.BlockSpec((tm, d), lambda i: (i, 0)),
        out_shape=jax.ShapeDtypeStruct((n, d), F32),
        compiler_params=_params("parallel"),
        name="out_proj")(o_fox, o_gdn, o_mem, p, p, p, p, p, p, x, wf, wg, wm, wo, gdn_gain)


def _split_w_in(w):
    o = np.cumsum([0, FOX_W, FOX_W, FOX_W, FOX_W, H_FOX, 3 * GDN_W, GDN_W, H_GDN, H_GDN, MEM_W, MEM_W])
    big = jnp.concatenate([w[:, o[0]:o[4]], w[:, o[5]:o[7]], w[:, o[9]:]], axis=1).astype(BF16)
    small = jnp.concatenate([w[:, o[4]:o[5]], w[:, o[7]:o[9]]], axis=1)
    small = jnp.pad(small, ((0, 0), (0, LANES - small.shape[1]))).astype(BF16)
    return big, small


def kernel(x_prompt, x_sample, cache_fox_k, cache_fox_v, cache_fox_logf, state_gdn, state_gdn_conv,
           cache_mem_k, cache_mem_v, page_table, mem_prompt, ln_gain, w_in, fox_q_gain, fox_k_gain,
           fox_f_bias, gdn_conv_w, gdn_A_log, gdn_dt_bias, gdn_out_gain, mem_norm_gain, w_mem_kv,
           mem_q_gain, mem_k_gain, w_fox_br, w_gdn_br, w_mem_br, w_out):
    b, t, d = x_prompt.shape
    bd_, td, _ = x_sample.shape
    depth, n_pool, page = cache_fox_k.shape[:3]
    n_pages = page_table.shape[1]
    n_mem = mem_prompt.shape[1]
    c = GDN_CHUNK
    assert page == LANES and t % c == 0 and td <= c and td % 8 == 0

    kc = jnp.transpose(cache_fox_k, (0, 1, 3, 4, 2)).reshape(depth * n_pool, FOX_W, page)
    vc = jnp.transpose(cache_fox_v, (0, 1, 3, 4, 2)).reshape(depth * n_pool, FOX_W, page)
    lft = jnp.swapaxes(cache_fox_logf, 2, 3).reshape(depth * n_pool * H_FOX, page)
    r3, s3 = [z.reshape(depth * n_pool, H_FOX, page) for z in _page_logf(lft)]

    yp = x_prompt.reshape(b * t, d)
    ys = x_sample.reshape(bd_ * td, d)
    eye = jnp.eye(H_FOX, dtype=F32)
    outs = [[] for _ in range(12)]
    for l in range(depth):
        w_big, w_small = _split_w_in(w_in[l])
        gain = ln_gain[l][None, :].astype(F32)
        wf, wg, wm, wo = [z[l].astype(BF16) for z in (w_fox_br, w_gdn_br, w_mem_br, w_out)]
        gdn_gain = gdn_out_gain[l][None, :].astype(F32)
        mq_gain = mem_q_gain[l][None, :].astype(F32)

        p, ps = _in_proj(yp, gain, w_big, w_small, min(1024, b * t))
        kn, fv, lf, qa, ka, va = _fox_prep_prompt(p, ps, fox_q_gain[l], fox_k_gain[l], fox_f_bias[l], b, t)
        o_fox = _fox_attn_prompt(qa, ka, va, b, t).reshape(b * t, FOX_W)
        gq, gk, gv, gg = _gdn_prep(p, ps, jnp.zeros((b, 8, 3 * GDN_W), F32), gdn_conv_w[l], gdn_A_log[l],
                                   gdn_dt_bias[l], b, t)
        o_gdn, s_new = _gdn(gq, gk, gv, gg, jnp.zeros((b, H_GDN, DK_GDN, DV_GDN), F32), b, t)
        mk, mv = _mem_kv(mem_prompt.reshape(b * n_mem, d), mem_norm_gain[l][None, :].astype(F32),
                         w_mem_kv[l].astype(BF16), mem_k_gain[l][None, :].astype(F32))
        o_mem = _mem_attn(p, mk.reshape(b, n_mem, MEM_W), mv.reshape(b, n_mem, MEM_W), mq_gain, b, t)
        conv_tail = p.reshape(b, t, -1)[:, t - (CONV_W - 1):, GQ * FOX_W:GQ * FOX_W + 3 * GDN_W]
        yp = _out_proj(o_fox, o_gdn, o_mem, p, yp, wf, wg, wm, wo, gdn_gain)
        for lst, val in zip(outs[:7], (kn.reshape(b, t, H_FOX, DH_FOX), fv.reshape(b, t, H_FOX, DH_FOX),
                                       lf[:, :, :H_FOX], s_new, conv_tail,
                                       mk.reshape(b, n_mem, H_MEM, DH_MEM), mv.reshape(b, n_mem, H_MEM, DH_MEM))):
            lst.append(val)

        p, ps = _in_proj(ys, gain, w_big, w_small, bd_ * td)
        qn, kn, lf, cn = _fox_sample_prep(p, ps, fox_q_gain[l], fox_k_gain[l], fox_f_bias[l], td)
        fv = p[:, FV * FOX_W:(FV + 1) * FOX_W]
        q4 = qn.reshape(bd_, td, H_FOX, DH_FOX).transpose(0, 2, 1, 3)
        qbd = (q4[:, :, :, None, :] * eye[None, :, None, :, None]).reshape(bd_, H_FOX * td, FOX_W).astype(BF16)
        cn3 = cn.reshape(bd_, td, LANES)[:, :, :H_FOX].transpose(0, 2, 1)
        cnq = jnp.broadcast_to(cn3.reshape(bd_, H_FOX * td, 1), (bd_, H_FOX * td, LANES))
        cnk = jnp.broadcast_to(cn3[:, :, None, :], (bd_, H_FOX, td, td)).reshape(bd_, H_FOX * td, td)
        cnk = jnp.pad(cnk, ((0, 0), (0, 0), (0, LANES - td)))
        pad_rows = lambda z: jnp.pad(z.reshape(bd_, td, FOX_W), ((0, 0), (0, page - td), (0, 0))).astype(BF16).transpose(0, 2, 1)
        pidx = (l * n_pool + page_table).reshape(-1).astype(jnp.int32)
        o_fox = _fox_decode(pidx, qbd, cnq, cnk, pad_rows(kn), pad_rows(fv), kc, vc, r3, s3, n_pages, td)
        o_fox = o_fox.reshape(bd_ * td, FOX_W)

        init = jnp.pad(state_gdn_conv[l].astype(F32), ((0, 0), (8 - (CONV_W - 1), 0), (0, 0)))
        gq, gk, gv, gg = _gdn_prep(p, ps, init, gdn_conv_w[l], gdn_A_log[l], gdn_dt_bias[l], bd_, td)
        padc = lambda z: jnp.pad(z.reshape(bd_, td, -1), ((0, 0), (0, c - td), (0, 0))).reshape(bd_ * c, -1)
        o_gdn, s_new = _gdn(padc(gq), padc(gk), padc(gv), padc(gg), state_gdn[l].astype(F32), bd_, c)
        o_gdn = o_gdn.reshape(bd_, c, GDN_W)[:, :td].reshape(bd_ * td, GDN_W)
        o_mem = _mem_attn(p, cache_mem_k[l].reshape(bd_, n_mem, MEM_W), cache_mem_v[l].reshape(bd_, n_mem, MEM_W),
                          mq_gain, bd_, td)
        conv_tail = jnp.concatenate([state_gdn_conv[l].astype(F32),
                                     p.reshape(bd_, td, -1)[:, :, GQ * FOX_W:GQ * FOX_W + 3 * GDN_W]],
                                    axis=1)[:, td:]
        ys = _out_proj(o_fox, o_gdn, o_mem, p, ys, wf, wg, wm, wo, gdn_gain)
        for lst, val in zip(outs[7:], (kn.reshape(bd_, td, H_FOX, DH_FOX), fv.reshape(bd_, td, H_FOX, DH_FOX),
                                       lf.reshape(bd_, td, LANES)[:, :, :H_FOX], s_new, conv_tail)):
            lst.append(val)

    return (yp.reshape(b, t, d), ys.reshape(bd_, td, d)) + tuple(jnp.stack(z) for z in outs)
```

```python
import functools

import numpy as np
import jax
import jax.numpy as jnp
from jax import lax
from jax.experimental import pallas as pl
from jax.experimental.pallas import tpu as pltpu

F32, BF16 = jnp.float32, jnp.bfloat16

H_FOX, DH_FOX = 8, 64
FOX_W = H_FOX * DH_FOX
H_GDN, DK_GDN, DV_GDN = 4, 128, 128
GDN_W = H_GDN * DV_GDN
CONV_W = 4
GDN_CHUNK = 64
H_MEM, DH_MEM = 4, 128
MEM_W = H_MEM * DH_MEM
N_BRANCH = 3
EPS = 1e-6
NEG = -1e30
LANES = 128
VMEM_LIMIT = 56 * 1024 * 1024
DECODE_PAGES_PER_STEP = 16

FQ, FK, FV, FZ, GQ, GK, GV, GZ, MQ, MZ = range(10)
GATE0 = 5
SM_FF, SM_GA, SM_GB = 0, 8, 12


def _params(*sem):
    return pltpu.CompilerParams(dimension_semantics=sem, vmem_limit_bytes=VMEM_LIMIT)


def _dot(a, b):
    return jnp.dot(a, b, preferred_element_type=F32)


def _dot_nt(a, b):
    return lax.dot_general(a, b, (((1,), (1,)), ((), ())), preferred_element_type=F32)


def _dot_tn(a, b):
    return lax.dot_general(a, b, (((0,), (0,)), ((), ())), preferred_element_type=F32)


def _split2(x):
    hi = x.astype(BF16)
    lo = (x - hi.astype(F32)).astype(BF16)
    return hi, lo


def _split3(x):
    hi = x.astype(BF16)
    r = x - hi.astype(F32)
    mid = r.astype(BF16)
    lo = (r - mid.astype(F32)).astype(BF16)
    return hi, mid, lo


def _dot3_lhs_exact(m, x):
    hi, mid, lo = _split3(x)
    return _dot(m, hi) + _dot(m, mid) + _dot(m, lo)


def _dot3_rhs_exact(x, m):
    hi, mid, lo = _split3(x)
    return _dot(hi, m) + _dot(mid, m) + _dot(lo, m)


def _sigmoid(x):
    return 1.0 / (1.0 + jnp.exp(-x))


def _silu(x):
    return x * _sigmoid(x)


def _softplus(x):
    return jnp.maximum(x, 0.0) + jnp.log1p(jnp.exp(-jnp.abs(x)))


def _log_sigmoid(x):
    return jnp.minimum(x, 0.0) - jnp.log1p(jnp.exp(-jnp.abs(x)))


def _lane_bcast(x, lane):
    return jnp.broadcast_to(x[:, lane:lane + 1], x.shape)


def _headnorm128(x, gain, nheads):
    outs = []
    for h in range(nheads):
        seg = x[:, h * LANES:(h + 1) * LANES]
        ms = jnp.mean(seg * seg, axis=-1, keepdims=True)
        outs.append(seg * lax.rsqrt(ms + EPS) * gain)
    return jnp.concatenate(outs, axis=1)


def _in_proj_kernel(x_ref, g_ref, w_ref, ws_ref, p_ref, ps_ref, h_ref):
    @pl.when(pl.program_id(1) == 0)
    def _():
        x = x_ref[...]
        ms = jnp.mean(x * x, axis=-1, keepdims=True)
        h = (x * lax.rsqrt(ms + EPS) * g_ref[...]).astype(BF16)
        h_ref[...] = h
        ps_ref[...] = _dot(h, ws_ref[...])

    p_ref[...] = _dot(h_ref[...], w_ref[...])


def _in_proj(x, gain, w_big, w_small, tm):
    n, d = x.shape
    nb = w_big.shape[1]
    tn = 1024
    return pl.pallas_call(
        _in_proj_kernel,
        grid=(n // tm, nb // tn),
        in_specs=[pl.BlockSpec((tm, d), lambda i, j: (i, 0)),
                  pl.BlockSpec((1, d), lambda i, j: (0, 0)),
                  pl.BlockSpec((d, tn), lambda i, j: (0, j)),
                  pl.BlockSpec((d, LANES), lambda i, j: (0, 0))],
        out_specs=[pl.BlockSpec((tm, tn), lambda i, j: (i, j)),
                   pl.BlockSpec((tm, LANES), lambda i, j: (i, 0))],
        out_shape=[jax.ShapeDtypeStruct((n, nb), F32), jax.ShapeDtypeStruct((n, LANES), F32)],
        scratch_shapes=[pltpu.VMEM((tm, d), BF16)],
        compiler_params=_params("parallel", "arbitrary"),
        name="in_proj")(x, gain, w_big, w_small)


def _fox_headnorm(x, gain, bd):
    hi, lo = _split2(x * x)
    ss = _dot(hi, bd) + _dot(lo, bd)
    return x * lax.rsqrt(ss * (1.0 / DH_FOX) + EPS) * gain


def _fox_prep_kernel(fq_ref, fk_ref, fv_ref, ps_ref, qg_ref, kg_ref, fb_ref, bd_ref, tri_ref,
                     selq_ref, selk_ref, selv_ref, qrow_ref, krow_ref,
                     kn_ref, v_ref, lf_ref, qa_ref, ka_ref, va_ref, carry_ref):
    @pl.when(pl.program_id(1) == 0)
    def _():
        carry_ref[...] = jnp.zeros_like(carry_ref)

    bd = bd_ref[...]
    qn = _fox_headnorm(fq_ref[...], qg_ref[...], bd) * (DH_FOX ** -0.5)
    kn = _fox_headnorm(fk_ref[...], kg_ref[...], bd)
    fv = fv_ref[...]
    kn_ref[0] = kn
    v_ref[0] = fv
    lf = _log_sigmoid(ps_ref[...] + fb_ref[...])
    lf_ref[0] = lf
    c = _dot3_lhs_exact(tri_ref[...], lf) + carry_ref[0:1, :]
    tm = c.shape[0]
    carry_ref[...] = jnp.broadcast_to(c[tm - 1:tm, :], carry_ref.shape)
    chi, cmid, clo = _split3(c)
    qa = _dot(jnp.concatenate([qn.astype(BF16), chi, cmid, clo], axis=1), selq_ref[...]) + qrow_ref[...]
    ka = _dot(jnp.concatenate([kn.astype(BF16), chi, cmid, clo], axis=1), selk_ref[...]) + krow_ref[...]
    qa_ref[0] = qa.astype(BF16)
    ka_ref[0] = ka.astype(BF16)
    vt = _dot_nt(selv_ref[...], fv.astype(BF16))
    r = lax.broadcasted_iota(jnp.int32, vt.shape, 0) & (2 * LANES - 1)
    va_ref[0, 0] = jnp.where((r == DH_FOX) | (r == LANES), 1.0, vt).astype(BF16)


def _fox_consts(tm):
    a = np.arange(FOX_W)
    bd = (a[:, None] // DH_FOX == a[None, :] // DH_FOX).astype(np.float32)
    r = np.arange(tm)
    tri = (r[None, :] <= r[:, None]).astype(np.float32)
    aw = H_FOX * LANES
    selq = np.zeros((FOX_W + 3 * LANES, aw), np.float32)
    selk = np.zeros((FOX_W + 3 * LANES, aw), np.float32)
    selv = np.zeros((aw, FOX_W), np.float32)
    qrow = np.zeros((1, aw), np.float32)
    krow = np.zeros((1, aw), np.float32)
    for h in range(H_FOX):
        for d in range(DH_FOX):
            selq[h * DH_FOX + d, h * LANES + d] = 1.0
            selk[h * DH_FOX + d, h * LANES + d] = 1.0
            selv[h * LANES + (h % 2) * DH_FOX + d, h * DH_FOX + d] = 1.0
        for e in range(3):
            selq[FOX_W + e * LANES + h, h * LANES + DH_FOX + e] = 1.0
            selk[FOX_W + e * LANES + h, h * LANES + DH_FOX + 3 + e] = -1.0
            qrow[0, h * LANES + DH_FOX + 3 + e] = 1.0
            krow[0, h * LANES + DH_FOX + e] = 1.0
    bf = lambda z: jnp.asarray(z, BF16)
    return bf(bd), bf(tri), bf(selq), bf(selk), bf(selv), jnp.asarray(qrow), jnp.asarray(krow)


def _tile_gain(g, reps):
    return jnp.tile(g.astype(F32), reps)[None, :]


def _pad_row(vals, offset):
    row = jnp.zeros((LANES,), F32)
    return lax.dynamic_update_slice(row, vals.astype(F32), (offset,))[None, :]


def _fox_prep_prompt(p, ps, qgain, kgain, fbias, b, t):
    tm = min(512, t)
    nt = t // tm
    bd, tri, selq, selk, selv, qrow, krow = _fox_consts(tm)
    aw = H_FOX * LANES
    row = lambda blk: pl.BlockSpec((tm, FOX_W), lambda i, j, blk=blk: (i * nt + j, blk))
    const = lambda shape: pl.BlockSpec(shape, lambda i, j: (0,) * len(shape))
    tok = lambda w: pl.BlockSpec((1, tm, w), lambda i, j: (i, j, 0))
    return pl.pallas_call(
        _fox_prep_kernel,
        grid=(b, nt),
        in_specs=[row(FQ), row(FK), row(FV),
                  pl.BlockSpec((tm, LANES), lambda i, j: (i * nt + j, 0)),
                  const((1, FOX_W)), const((1, FOX_W)), const((1, LANES)),
                  const(bd.shape), const(tri.shape), const(selq.shape), const(selk.shape), const(selv.shape),
                  const(qrow.shape), const(krow.shape)],
        out_specs=[tok(FOX_W), tok(FOX_W), tok(LANES), tok(aw), tok(aw),
                   pl.BlockSpec((1, 1, aw, tm), lambda i, j: (i, j, 0, 0))],
        out_shape=[jax.ShapeDtypeStruct((b, t, FOX_W), F32), jax.ShapeDtypeStruct((b, t, FOX_W), F32),
                   jax.ShapeDtypeStruct((b, t, LANES), F32),
                   jax.ShapeDtypeStruct((b, t, aw), BF16), jax.ShapeDtypeStruct((b, t, aw), BF16),
                   jax.ShapeDtypeStruct((b, nt, aw, tm), BF16)],
        scratch_shapes=[pltpu.VMEM((8, LANES), F32)],
        compiler_params=_params("parallel", "arbitrary"),
        name="fox_prep")(p, p, p, ps, _tile_gain(qgain, H_FOX), _tile_gain(kgain, H_FOX), _pad_row(fbias, SM_FF),
                         bd, tri, selq, selk, selv, qrow, krow)


def _fox_attn_kernel(q_ref, k_ref, vt_ref, o_ref, m_ref, acc_ref, *, tk):
    qi = pl.program_id(2)
    tq = q_ref.shape[1]
    krow = lax.broadcasted_iota(jnp.int32, (tk, tq), 0)
    qcol = lax.broadcasted_iota(jnp.int32, (tk, tq), 1)
    m_ref[...] = jnp.full(m_ref.shape, NEG, F32)
    acc_ref[...] = jnp.zeros_like(acc_ref)

    def step(ki, masked):
        r0 = pl.multiple_of(ki * tk, tk)
        for hh in range(2):
            lanes = slice(hh * LANES, (hh + 1) * LANES)
            st = _dot_nt(k_ref[0, pl.ds(r0, tk), lanes], q_ref[0, :, lanes])
            if masked:
                st = jnp.where(krow <= qcol, st, NEG)
            m_prev = m_ref[hh]
            m_new = jnp.maximum(m_prev, jnp.max(st, axis=0, keepdims=True))
            pt = jnp.exp(st - m_new).astype(BF16)
            acc_ref[hh] = jnp.exp(m_prev - m_new) * acc_ref[hh] + _dot(vt_ref[0, ki, lanes, :], pt)
            m_ref[hh] = m_new

    def body(ki, carry):
        step(ki, False)
        return carry

    lax.fori_loop(0, qi, body, 0)
    step(qi, True)
    a0, a1 = acc_ref[0], acc_ref[1]
    row = lax.broadcasted_iota(jnp.int32, a0.shape, 0)
    ot = jnp.where(row < DH_FOX, a0 / a0[DH_FOX:DH_FOX + 1, :], a1 / a1[0:1, :])
    o_ref[0] = ot.T


def _fox_attn_prompt(qa, ka, vat, b, t):
    tq = vat.shape[3]
    pair = 2 * LANES
    return pl.pallas_call(
        functools.partial(_fox_attn_kernel, tk=tq),
        grid=(b, H_FOX // 2, t // tq),
        in_specs=[pl.BlockSpec((1, tq, pair), lambda i, h, q: (i, q, h)),
                  pl.BlockSpec((1, t, pair), lambda i, h, q: (i, 0, h)),
                  pl.BlockSpec((1, t // tq, pair, tq), lambda i, h, q: (i, 0, h, 0))],
        out_specs=pl.BlockSpec((1, tq, LANES), lambda i, h, q: (i, q, h)),
        out_shape=jax.ShapeDtypeStruct((b, t, FOX_W), F32),
        scratch_shapes=[pltpu.VMEM((2, 1, tq), F32), pltpu.VMEM((2, LANES, tq), F32)],
        compiler_params=_params("parallel", "parallel", "arbitrary"),
        name="fox_attn")(qa, ka, vat)


def _fox_sample_prep_kernel(fq_ref, fk_ref, ps_ref, qg_ref, kg_ref, fb_ref, bd_ref, tri_ref,
                            qn_ref, kn_ref, lf_ref, cn_ref):
    bd = bd_ref[...]
    qn_ref[...] = _fox_headnorm(fq_ref[...], qg_ref[...], bd) * (DH_FOX ** -0.5)
    kn_ref[...] = _fox_headnorm(fk_ref[...], kg_ref[...], bd)
    lf = _log_sigmoid(ps_ref[...] + fb_ref[...])
    lf_ref[...] = lf
    cn_ref[...] = _dot3_lhs_exact(tri_ref[...], lf)


def _fox_sample_prep(p, ps, qgain, kgain, fbias, td):
    n = p.shape[0]
    a = np.arange(FOX_W)
    bd = jnp.asarray((a[:, None] // DH_FOX == a[None, :] // DH_FOX), BF16)
    r = np.arange(n)
    tri = jnp.asarray((r[None, :] <= r[:, None]) & (r[None, :] // td == r[:, None] // td), BF16)
    full = lambda shape: pl.BlockSpec(shape, lambda i: (0,) * len(shape))
    return pl.pallas_call(
        _fox_sample_prep_kernel,
        grid=(1,),
        in_specs=[pl.BlockSpec((n, FOX_W), lambda i: (0, FQ)), pl.BlockSpec((n, FOX_W), lambda i: (0, FK)),
                  full((n, LANES)), full((1, FOX_W)), full((1, FOX_W)), full((1, LANES)),
                  full(bd.shape), full(tri.shape)],
        out_specs=[full((n, FOX_W)), full((n, FOX_W)), full((n, LANES)), full((n, LANES))],
        out_shape=[jax.ShapeDtypeStruct((n, FOX_W), F32), jax.ShapeDtypeStruct((n, FOX_W), F32),
                   jax.ShapeDtypeStruct((n, LANES), F32), jax.ShapeDtypeStruct((n, LANES), F32)],
        compiler_params=_params("arbitrary"),
        name="fox_sample_prep")(p, p, ps, _tile_gain(qgain, H_FOX), _tile_gain(kgain, H_FOX),
                                _pad_row(fbias, SM_FF), bd, tri)


def _page_logf_kernel(x_ref, u_ref, ones_ref, r_ref, s_ref):
    x = x_ref[...]
    r_ref[...] = _dot3_rhs_exact(x, u_ref[...])
    s_ref[...] = _dot3_rhs_exact(x, ones_ref[...])


def _page_logf(xt):
    n, page = xt.shape
    tm = 2048
    while n % tm:
        tm //= 2
    j = np.arange(page)
    u = jnp.asarray(j[:, None] > j[None, :], BF16)
    ones = jnp.ones((page, page), BF16)
    return pl.pallas_call(
        _page_logf_kernel,
        grid=(n // tm,),
        in_specs=[pl.BlockSpec((tm, page), lambda i: (i, 0)),
                  pl.BlockSpec((page, page), lambda i: (0, 0)), pl.BlockSpec((page, page), lambda i: (0, 0))],
        out_specs=[pl.BlockSpec((tm, page), lambda i: (i, 0)), pl.BlockSpec((tm, page), lambda i: (i, 0))],
        out_shape=[jax.ShapeDtypeStruct((n, page), F32), jax.ShapeDtypeStruct((n, page), F32)],
        compiler_params=_params("parallel"),
        name="page_logf")(xt, u, ones)


def _fox_decode_kernel(pidx_ref, q_ref, cn_ref, cnk_ref, knew_ref, vnew_ref, *rest, g, td):
    k_refs, v_refs, r_refs, s_refs = rest[0:g], rest[g:2 * g], rest[2 * g:3 * g], rest[3 * g:4 * g]
    o_ref, m_ref, l_ref, acc_ref, suf_ref = rest[4 * g:]
    p = pl.program_id(1)
    q = q_ref[0]
    cn = cn_ref[0]
    nrow = q.shape[0]

    @pl.when(p == 0)
    def _():
        s = _dot(q, knew_ref[0])
        qpos = lax.broadcasted_iota(jnp.int32, s.shape, 0) % td
        kpos = lax.broadcasted_iota(jnp.int32, s.shape, 1)
        s = jnp.where(kpos <= qpos, s + cn - cnk_ref[0], NEG)
        m = jnp.max(s, axis=-1, keepdims=True)
        pr = jnp.exp(s - m)
        m_ref[...] = m
        l_ref[...] = jnp.sum(pr, axis=-1, keepdims=True)
        acc_ref[...] = _dot_nt(pr.astype(BF16), vnew_ref[0])
        suf_ref[...] = jnp.zeros_like(suf_ref)

    suf = suf_ref[...]
    scores = []
    for j in range(g):
        z = r_refs[j][0] + suf
        bias = jnp.concatenate([jnp.broadcast_to(z[h:h + 1, :], (td, LANES)) for h in range(H_FOX)], axis=0)
        scores.append(_dot(q, k_refs[j][0].astype(BF16)) + bias + cn)
        suf = suf + s_refs[j][0]
    suf_ref[...] = suf
    m_prev = m_ref[...]
    m_new = m_prev
    for s in scores:
        m_new = jnp.maximum(m_new, jnp.max(s, axis=-1, keepdims=True))
    alpha = jnp.exp(m_prev - m_new)
    l = alpha * l_ref[...]
    acc = alpha * acc_ref[...]
    for j in range(g):
        pr = jnp.exp(scores[j] - m_new)
        l = l + jnp.sum(pr, axis=-1, keepdims=True)
        acc = acc + _dot_nt(pr.astype(BF16), v_refs[j][0].astype(BF16))
    m_ref[...] = m_new
    l_ref[...] = l
    acc_ref[...] = acc

    @pl.when(p == pl.num_programs(1) - 1)
    def _():
        o = acc / l
        head = lax.broadcasted_iota(jnp.int32, (td, FOX_W), 1) // DH_FOX
        out = jnp.zeros((td, FOX_W), F32)
        for h in range(H_FOX):
            out = out + jnp.where(head == h, o[h * td:(h + 1) * td, :], 0.0)
        o_ref[0] = out


def _fox_decode(pidx, qbd, cn, cnk, knew, vnew, kc, vc, r3, s3, n_pages, td):
    bd_, nrow, _ = qbd.shape
    page = kc.shape[2]
    g = min(DECODE_PAGES_PER_STEP, n_pages)
    steps = n_pages // g

    def page_map(j):
        return lambda b, p, idx: (idx[b * n_pages + (n_pages - 1 - (p * g + j))], 0, 0)

    per_b = lambda shape: pl.BlockSpec((1,) + shape, lambda b, p, idx: (b, 0, 0))
    in_specs = [per_b((nrow, FOX_W)), per_b((nrow, LANES)), per_b((nrow, LANES)),
                per_b((FOX_W, page)), per_b((FOX_W, page))]
    in_specs += [pl.BlockSpec((1, FOX_W, page), page_map(j)) for j in range(g)]
    in_specs += [pl.BlockSpec((1, FOX_W, page), page_map(j)) for j in range(g)]
    in_specs += [pl.BlockSpec((1, H_FOX, LANES), page_map(j)) for j in range(g)]
    in_specs += [pl.BlockSpec((1, H_FOX, LANES), page_map(j)) for j in range(g)]
    grid_spec = pltpu.PrefetchScalarGridSpec(
        num_scalar_prefetch=1, grid=(bd_, steps), in_specs=in_specs,
        out_specs=pl.BlockSpec((1, td, FOX_W), lambda b, p, idx: (b, 0, 0)),
        scratch_shapes=[pltpu.VMEM((nrow, 1), F32), pltpu.VMEM((nrow, 1), F32), pltpu.VMEM((nrow, FOX_W), F32),
                        pltpu.VMEM((H_FOX, LANES), F32)])
    return pl.pallas_call(
        functools.partial(_fox_decode_kernel, g=g, td=td),
        grid_spec=grid_spec,
        out_shape=jax.ShapeDtypeStruct((bd_, td, FOX_W), F32),
        compiler_params=_params("parallel", "arbitrary"),
        name="fox_decode")(pidx, qbd, cn, cnk, knew, vnew, *([kc] * g), *([vc] * g), *([r3] * g), *([s3] * g))


def _gdn_prep_kernel(gq_ref, gk_ref, gv_ref, ps_ref, init_ref, w_ref, alog_ref, dtb_ref,
                     q_ref, k_ref, v_ref, g_ref, ext_ref):
    tm = gq_ref.shape[0]

    @pl.when(pl.program_id(1) == 0)
    def _():
        ext_ref[0:8, :] = init_ref[0]

    @pl.when(pl.program_id(1) > 0)
    def _():
        ext_ref[0:8, :] = ext_ref[tm:tm + 8, :]

    ext_ref[8:8 + tm, 0:GDN_W] = gq_ref[...]
    ext_ref[8:8 + tm, GDN_W:2 * GDN_W] = gk_ref[...]
    ext_ref[8:8 + tm, 2 * GDN_W:3 * GDN_W] = gv_ref[...]
    conv = ext_ref[8:8 + tm, :] * w_ref[CONV_W - 1:CONV_W, :]
    for i in range(CONV_W - 1):
        off = 8 - (CONV_W - 1) + i
        conv = conv + ext_ref[off:off + tm, :] * w_ref[i:i + 1, :]
    s = _silu(conv)

    def l2(x):
        outs = []
        for h in range(H_GDN):
            seg = x[:, h * LANES:(h + 1) * LANES]
            outs.append(seg * lax.rsqrt(jnp.sum(seg * seg, axis=-1, keepdims=True) + EPS))
        return jnp.concatenate(outs, axis=1)

    q_ref[...] = l2(s[:, 0:GDN_W]) * (DK_GDN ** -0.5)
    k_ref[...] = l2(s[:, GDN_W:2 * GDN_W])
    v_ref[...] = s[:, 2 * GDN_W:3 * GDN_W]
    ps = ps_ref[...]
    lane = lax.broadcasted_iota(jnp.int32, ps.shape, 1)
    gate = -jnp.exp(alog_ref[...]) * _softplus(ps + dtb_ref[...])
    beta = _sigmoid(ps)
    g_ref[...] = jnp.where((lane >= SM_GA) & (lane < SM_GB), gate,
                           jnp.where((lane >= SM_GB) & (lane < SM_GB + H_GDN), beta, 0.0))


def _gdn_prep(p, ps, init, conv_w, a_log, dt_bias, b, t):
    tm = min(512, t)
    nt = t // tm
    n = b * t
    row = lambda blk: pl.BlockSpec((tm, GDN_W), lambda i, j, blk=blk: (i * nt + j, blk))
    out = pl.BlockSpec((tm, GDN_W), lambda i, j: (i * nt + j, 0))
    small = pl.BlockSpec((tm, LANES), lambda i, j: (i * nt + j, 0))
    const = lambda shape: pl.BlockSpec(shape, lambda i, j: (0,) * len(shape))
    return pl.pallas_call(
        _gdn_prep_kernel,
        grid=(b, nt),
        in_specs=[row(GQ), row(GK), row(GV), small,
                  pl.BlockSpec((1, 8, 3 * GDN_W), lambda i, j: (i, 0, 0)),
                  const((CONV_W, 3 * GDN_W)), const((1, LANES)), const((1, LANES))],
        out_specs=[out, out, out, small],
        out_shape=[jax.ShapeDtypeStruct((n, GDN_W), F32)] * 3 + [jax.ShapeDtypeStruct((n, LANES), F32)],
        scratch_shapes=[pltpu.VMEM((tm + 8, 3 * GDN_W), F32)],
        compiler_params=_params("parallel", "arbitrary"),
        name="gdn_prep")(p, p, p, ps, init, conv_w.astype(F32), _pad_row(a_log, SM_GA), _pad_row(dt_bias, SM_GA))


def _gdn_chunk_kernel(q_ref, k_ref, v_ref, g_ref, tri_ref,
                      m_ref, qk_ref, u_ref, w_ref, qg_ref, kg_ref, egl_ref, *, nch):
    c = GDN_CHUNK
    tri = tri_ref[...]
    ri = lax.broadcasted_iota(jnp.int32, (c, c), 0)
    ci = lax.broadcasted_iota(jnp.int32, (c, c), 1)
    lane = lax.broadcasted_iota(jnp.int32, (c, LANES), 1)
    for cc in range(nch):
        rows = slice(cc * c, (cc + 1) * c)
        gt = g_ref[rows, :]
        gcum = _dot3_lhs_exact(tri, gt)
        for h in range(H_GDN):
            cols = slice(h * LANES, (h + 1) * LANES)
            gcol = _lane_bcast(gcum, SM_GA + h)
            beta = _lane_bcast(gt, SM_GB + h)
            hi, mid, lo = [z.astype(F32) for z in _split3(gcol)]
            a = jnp.where(lane == 0, hi, jnp.where(lane == 1, mid, jnp.where(lane == 2, lo,
                          jnp.where(lane < 6, 1.0, 0.0))))
            bm = jnp.where(lane < 3, 1.0, jnp.where(lane == 3, -hi, jnp.where(lane == 4, -mid,
                           jnp.where(lane == 5, -lo, 0.0))))
            diff = _dot_nt(a.astype(BF16), bm.astype(BF16))
            decay = jnp.exp(jnp.where(ci <= ri, diff, NEG))
            qh, kh, vh = q_ref[rows, cols], k_ref[rows, cols], v_ref[rows, cols]
            kb = kh * beta
            khb = kh.astype(BF16)
            m_ref[cc * H_GDN + h] = jnp.where(ci < ri, _dot_nt(kb.astype(BF16), khb) * decay, 0.0)
            qk_ref[cc * H_GDN + h] = _dot_nt(qh.astype(BF16), khb) * decay
            eg = jnp.exp(gcol)
            glast = gcol[c - 1:c, :]
            u_ref[rows, cols] = vh * beta
            w_ref[rows, cols] = kb * eg
            qg_ref[rows, cols] = qh * eg
            kg_ref[rows, cols] = kh * jnp.exp(glast - gcol)
            egl_ref[cc * H_GDN + h] = jnp.broadcast_to(jnp.exp(glast), (8, LANES))


def _gdn_chunk(q, k, v, g):
    n = q.shape[0]
    c = GDN_CHUNK
    tm = min(256, n)
    nch = tm // c
    r = np.arange(c)
    tri = jnp.asarray(r[None, :] <= r[:, None], BF16)
    nc = n // c * H_GDN
    row = pl.BlockSpec((tm, GDN_W), lambda i: (i, 0))
    blk = lambda shape: pl.BlockSpec((nch * H_GDN,) + shape, lambda i: (i, 0, 0))
    return pl.pallas_call(
        functools.partial(_gdn_chunk_kernel, nch=nch),
        grid=(n // tm,),
        in_specs=[row, row, row, pl.BlockSpec((tm, LANES), lambda i: (i, 0)), pl.BlockSpec((c, c), lambda i: (0, 0))],
        out_specs=[blk((c, c)), blk((c, c)), row, row, row, row, blk((8, LANES))],
        out_shape=[jax.ShapeDtypeStruct((nc, c, c), F32), jax.ShapeDtypeStruct((nc, c, c), F32)]
                  + [jax.ShapeDtypeStruct((n, GDN_W), F32)] * 4 + [jax.ShapeDtypeStruct((nc, 8, LANES), F32)],
        compiler_params=_params("parallel"),
        name="gdn_chunk")(q, k, v, g, tri)


def _gdn_inv_kernel(m_ref, x_ref):
    c = GDN_CHUNK
    for i in range(c):
        ext = 8 * (i // 8 + 1)
        e_i = (lax.broadcasted_iota(jnp.int32, (ext, LANES), 0) == i).astype(F32)

        def body(j, acc, i=i, ext=ext):
            return acc - m_ref[i, pl.ds(j, 1), :] * x_ref[j, 0:ext, :]

        x_ref[i, 0:ext, :] = lax.fori_loop(0, i, body, e_i, unroll=min(max(i, 1), 8))
        if ext < c:
            x_ref[i, ext:c, :] = jnp.zeros((c - ext, LANES), F32)


def _gdn_inv(mt):
    c = GDN_CHUNK
    nc = mt.shape[2]
    spec = pl.BlockSpec((c, c, LANES), lambda i: (0, 0, i))
    return pl.pallas_call(
        _gdn_inv_kernel,
        grid=(nc // LANES,),
        in_specs=[spec], out_specs=spec,
        out_shape=jax.ShapeDtypeStruct((c, c, nc), F32),
        compiler_params=_params("parallel"),
        name="gdn_inv")(mt)


def _gdn_scan_kernel(x_ref, qk_ref, u_ref, w_ref, qg_ref, kg_ref, egl_ref, s0_ref, o_ref, sout_ref,
                     s_ref, us_ref, wq_ref, *, nch):
    c = GDN_CHUNK

    @pl.when(pl.program_id(1) == 0)
    def _():
        s_ref[...] = s0_ref[0]

    def solve(cc, carry):
        r0 = pl.multiple_of(cc * c, c)
        r2 = pl.multiple_of(cc * 2 * c, 2 * c)
        for h in range(H_GDN):
            cols = slice(h * LANES, (h + 1) * LANES)
            rhs = jnp.concatenate([u_ref[pl.ds(r0, c), cols], w_ref[pl.ds(r0, c), cols]], axis=1)
            xh, xl = _split2(x_ref[cc * H_GDN + h])
            rh, rl = _split2(rhs)
            sol = _dot(xh, rh) + _dot(xh, rl) + _dot(xl, rh)
            us_ref[pl.ds(r0, c), cols] = sol[:, 0:DV_GDN]
            wq_ref[pl.ds(r2, c), cols] = sol[:, DV_GDN:].astype(BF16)
            wq_ref[pl.ds(r2 + c, c), cols] = qg_ref[pl.ds(r0, c), cols].astype(BF16)
        return carry

    lax.fori_loop(0, nch, solve, 0)

    def body(cc, carry):
        r0 = pl.multiple_of(cc * c, c)
        r2 = pl.multiple_of(cc * 2 * c, 2 * c)
        heads = [slice(h * LANES, (h + 1) * LANES) for h in range(H_GDN)]
        sts = [s_ref[h] for h in range(H_GDN)]
        boths = [_dot(wq_ref[pl.ds(r2, 2 * c), heads[h]], sts[h].astype(BF16)) for h in range(H_GDN)]
        dbs = [(us_ref[pl.ds(r0, c), heads[h]] - boths[h][0:c]).astype(BF16) for h in range(H_GDN)]
        for h in range(H_GDN):
            egl = jnp.broadcast_to(egl_ref[cc * H_GDN + h][0:1, :], (DK_GDN, DV_GDN))
            s_ref[h] = sts[h] * egl + _dot_tn(kg_ref[pl.ds(r0, c), heads[h]].astype(BF16), dbs[h])
        for h in range(H_GDN):
            o_ref[pl.ds(r0, c), heads[h]] = boths[h][c:] + _dot(qk_ref[cc * H_GDN + h].astype(BF16), dbs[h])
        return carry

    lax.fori_loop(0, nch, body, 0)

    @pl.when(pl.program_id(1) == pl.num_programs(1) - 1)
    def _():
        sout_ref[0] = s_ref[...]


def _gdn_scan(x, qk, u, w, qg, kg, egl, s0, b, t):
    c = GDN_CHUNK
    tg = min(1024, t)
    ng = t // tg
    nch = tg // c
    row = pl.BlockSpec((tg, GDN_W), lambda i, j: (i * ng + j, 0))
    blk = lambda shape: pl.BlockSpec((nch * H_GDN,) + shape, lambda i, j: (i * ng + j, 0, 0))
    st = pl.BlockSpec((1, H_GDN, DK_GDN, DV_GDN), lambda i, j: (i, 0, 0, 0))
    return pl.pallas_call(
        functools.partial(_gdn_scan_kernel, nch=nch),
        grid=(b, ng),
        in_specs=[blk((c, c)), blk((c, c)), row, row, row, row, blk((8, LANES)), st],
        out_specs=[row, st],
        out_shape=[jax.ShapeDtypeStruct((b * t, GDN_W), F32), jax.ShapeDtypeStruct((b, H_GDN, DK_GDN, DV_GDN), F32)],
        scratch_shapes=[pltpu.VMEM((H_GDN, DK_GDN, DV_GDN), F32), pltpu.VMEM((tg, GDN_W), F32),
                        pltpu.VMEM((2 * tg, GDN_W), BF16)],
        compiler_params=_params("parallel", "arbitrary"),
        name="gdn_scan")(x, qk, u, w, qg, kg, egl, s0)


def _gdn(q, k, v, g, s0, b, t):
    c = GDN_CHUNK
    m, qk, u, w, qg, kg, egl = _gdn_chunk(q, k, v, g)
    nc = m.shape[0]
    pad = (-nc) % LANES
    mt = jnp.pad(m.reshape(nc, c * c), ((0, pad), (0, 0))).T.reshape(c, c, nc + pad)
    x = _gdn_inv(mt).reshape(c * c, nc + pad).T[:nc].reshape(nc, c, c)
    return _gdn_scan(x, qk, u, w, qg, kg, egl, s0, b, t)


def _mem_kv_kernel(m_ref, g_ref, w_ref, kg_ref, mk_ref, mv_ref):
    x = m_ref[...]
    ms = jnp.mean(x * x, axis=-1, keepdims=True)
    h = (x * lax.rsqrt(ms + EPS) * g_ref[...]).astype(BF16)
    kv = _dot(h, w_ref[...])
    mk_ref[...] = _headnorm128(kv[:, 0:MEM_W], kg_ref[...], H_MEM)
    mv_ref[...] = kv[:, MEM_W:]


def _mem_kv(mem, gain, w_kv, k_gain):
    n, d = mem.shape
    tm = 256
    return pl.pallas_call(
        _mem_kv_kernel,
        grid=(n // tm,),
        in_specs=[pl.BlockSpec((tm, d), lambda i: (i, 0)), pl.BlockSpec((1, d), lambda i: (0, 0)),
                  pl.BlockSpec((d, 2 * MEM_W), lambda i: (0, 0)), pl.BlockSpec((1, LANES), lambda i: (0, 0))],
        out_specs=[pl.BlockSpec((tm, MEM_W), lambda i: (i, 0))] * 2,
        out_shape=[jax.ShapeDtypeStruct((n, MEM_W), F32)] * 2,
        compiler_params=_params("parallel"),
        name="mem_kv")(mem, gain, w_kv, k_gain)


def _mem_attn_kernel(q_ref, mk_ref, mv_ref, g_ref, o_ref, *, cast):
    dt = BF16 if cast else F32
    qn = _headnorm128(q_ref[...], g_ref[...], H_MEM)
    outs = []
    for h in range(H_MEM):
        cols = slice(h * LANES, (h + 1) * LANES)
        s = _dot_nt(qn[:, cols].astype(dt), mk_ref[0, :, cols].astype(dt)) * (DH_MEM ** -0.5)
        p = jnp.exp(s - jnp.max(s, axis=-1, keepdims=True))
        p = p / jnp.sum(p, axis=-1, keepdims=True)
        outs.append(_dot(p.astype(dt), mv_ref[0, :, cols].astype(dt)))
    o_ref[...] = jnp.concatenate(outs, axis=1)


def _mem_attn(p, mk, mv, q_gain, b, t):
    tm = min(512, t)
    nt = t // tm
    n_mem = mk.shape[1]
    kv = pl.BlockSpec((1, n_mem, MEM_W), lambda i, j: (i, 0, 0))
    return pl.pallas_call(
        functools.partial(_mem_attn_kernel, cast=tm >= 16),
        grid=(b, nt),
        in_specs=[pl.BlockSpec((tm, MEM_W), lambda i, j: (i * nt + j, MQ)), kv, kv,
                  pl.BlockSpec((1, LANES), lambda i, j: (0, 0))],
        out_specs=pl.BlockSpec((tm, MEM_W), lambda i, j: (i * nt + j, 0)),
        out_shape=jax.ShapeDtypeStruct((b * t, MEM_W), F32),
        compiler_params=_params("parallel", "parallel"),
        name="mem_attn")(p, mk, mv, q_gain)


def _out_kernel(of_ref, og_ref, om_ref, fz_ref, gz_ref, mz_ref, g0_ref, g1_ref, g2_ref, x_ref,
                wf_ref, wg_ref, wm_ref, wo_ref, gg_ref, y_ref):
    a = (of_ref[...] * _silu(fz_ref[...])).astype(BF16)
    b = (_headnorm128(og_ref[...], gg_ref[...], H_GDN) * _silu(gz_ref[...])).astype(BF16)
    c = (om_ref[...] * _silu(mz_ref[...])).astype(BF16)
    merged = (_sigmoid(g0_ref[...]) * _dot(a, wf_ref[...]) + _sigmoid(g1_ref[...]) * _dot(b, wg_ref[...])
              + _sigmoid(g2_ref[...]) * _dot(c, wm_ref[...]))
    y_ref[...] = x_ref[...] + _dot(merged.astype(BF16), wo_ref[...])


def _out_proj(o_fox, o_gdn, o_mem, p, x, wf, wg, wm, wo, gdn_gain):
    n, d = x.shape
    tm = min(256, n)
    branch = pl.BlockSpec((tm, FOX_W), lambda i: (i, 0))
    pz = lambda blk: pl.BlockSpec((tm, FOX_W), lambda i, blk=blk: (i, blk))
    pg = lambda blk: pl.BlockSpec((tm, d), lambda i, blk=blk: (i, blk))
    const = lambda shape: pl.BlockSpec(shape, lambda i: (0, 0))
    return pl.pallas_call(
        _out_kernel,
        grid=(n // tm,),
        in_specs=[branch, branch, branch, pz(FZ), pz(GZ), pz(MZ), pg(GATE0), pg(GATE0 + 1), pg(GATE0 + 2),
                  pl.BlockSpec((tm, d), lambda i: (i, 0)),
                  const(wf.shape), const(wg.shape), const(wm.shape), const(wo.shape), const((1, LANES))],
        out_specs=pl.BlockSpec((tm, d), lambda i: (i, 0)),
        out_shape=jax.ShapeDtypeStruct((n, d), F32),
        compiler_params=_params("parallel"),
        name="out_proj")(o_fox, o_gdn, o_mem, p, p, p, p, p, p, x, wf, wg, wm, wo, gdn_gain)


def _split_w_in(w):
    o = np.cumsum([0, FOX_W, FOX_W, FOX_W, FOX_W, H_FOX, 3 * GDN_W, GDN_W, H_GDN, H_GDN, MEM_W, MEM_W])
    big = jnp.concatenate([w[:, o[0]:o[4]], w[:, o[5]:o[7]], w[:, o[9]:]], axis=1).astype(BF16)
    small = jnp.concatenate([w[:, o[4]:o[5]], w[:, o[7]:o[9]]], axis=1)
    small = jnp.pad(small, ((0, 0), (0, LANES - small.shape[1]))).astype(BF16)
    return big, small


def kernel(x_prompt, x_sample, cache_fox_k, cache_fox_v, cache_fox_logf, state_gdn, state_gdn_conv,
           cache_mem_k, cache_mem_v, page_table, mem_prompt, ln_gain, w_in, fox_q_gain, fox_k_gain,
           fox_f_bias, gdn_conv_w, gdn_A_log, gdn_dt_bias, gdn_out_gain, mem_norm_gain, w_mem_kv,
           mem_q_gain, mem_k_gain, w_fox_br, w_gdn_br, w_mem_br, w_out):
    b, t, d = x_prompt.shape
    bd_, td, _ = x_sample.shape
    depth, n_pool, page = cache_fox_k.shape[:3]
    n_pages = page_table.shape[1]
    n_mem = mem_prompt.shape[1]
    c = GDN_CHUNK
    assert page == LANES and t % c == 0 and td <= c and td % 8 == 0

    kc = jnp.transpose(cache_fox_k, (0, 1, 3, 4, 2)).reshape(depth * n_pool, FOX_W, page)
    vc = jnp.transpose(cache_fox_v, (0, 1, 3, 4, 2)).reshape(depth * n_pool, FOX_W, page)
    lft = jnp.swapaxes(cache_fox_logf, 2, 3).reshape(depth * n_pool * H_FOX, page)
    r3, s3 = [z.reshape(depth * n_pool, H_FOX, page) for z in _page_logf(lft)]

    yp = x_prompt.reshape(b * t, d)
    ys = x_sample.reshape(bd_ * td, d)
    eye = jnp.eye(H_FOX, dtype=F32)
    outs = [[] for _ in range(12)]
    for l in range(depth):
        w_big, w_small = _split_w_in(w_in[l])
        gain = ln_gain[l][None, :].astype(F32)
        wf, wg, wm, wo = [z[l].astype(BF16) for z in (w_fox_br, w_gdn_br, w_mem_br, w_out)]
        gdn_gain = gdn_out_gain[l][None, :].astype(F32)
        mq_gain = mem_q_gain[l][None, :].astype(F32)

        p, ps = _in_proj(yp, gain, w_big, w_small, min(1024, b * t))
        kn, fv, lf, qa, ka, va = _fox_prep_prompt(p, ps, fox_q_gain[l], fox_k_gain[l], fox_f_bias[l], b, t)
        o_fox = _fox_attn_prompt(qa, ka, va, b, t).reshape(b * t, FOX_W)
        gq, gk, gv, gg = _gdn_prep(p, ps, jnp.zeros((b, 8, 3 * GDN_W), F32), gdn_conv_w[l], gdn_A_log[l],
                                   gdn_dt_bias[l], b, t)
        o_gdn, s_new = _gdn(gq, gk, gv, gg, jnp.zeros((b, H_GDN, DK_GDN, DV_GDN), F32), b, t)
        mk, mv = _mem_kv(mem_prompt.reshape(b * n_mem, d), mem_norm_gain[l][None, :].astype(F32),
                         w_mem_kv[l].astype(BF16), mem_k_gain[l][None, :].astype(F32))
        o_mem = _mem_attn(p, mk.reshape(b, n_mem, MEM_W), mv.reshape(b, n_mem, MEM_W), mq_gain, b, t)
        conv_tail = p.reshape(b, t, -1)[:, t - (CONV_W - 1):, GQ * FOX_W:GQ * FOX_W + 3 * GDN_W]
        yp = _out_proj(o_fox, o_gdn, o_mem, p, yp, wf, wg, wm, wo, gdn_gain)
        for lst, val in zip(outs[:7], (kn.reshape(b, t, H_FOX, DH_FOX), fv.reshape(b, t, H_FOX, DH_FOX),
                                       lf[:, :, :H_FOX], s_new, conv_tail,
                                       mk.reshape(b, n_mem, H_MEM, DH_MEM), mv.reshape(b, n_mem, H_MEM, DH_MEM))):
            lst.append(val)

        p, ps = _in_proj(ys, gain, w_big, w_small, bd_ * td)
        qn, kn, lf, cn = _fox_sample_prep(p, ps, fox_q_gain[l], fox_k_gain[l], fox_f_bias[l], td)
        fv = p[:, FV * FOX_W:(FV + 1) * FOX_W]
        q4 = qn.reshape(bd_, td, H_FOX, DH_FOX).transpose(0, 2, 1, 3)
        qbd = (q4[:, :, :, None, :] * eye[None, :, None, :, None]).reshape(bd_, H_FOX * td, FOX_W).astype(BF16)
        cn3 = cn.reshape(bd_, td, LANES)[:, :, :H_FOX].transpose(0, 2, 1)
        cnq = jnp.broadcast_to(cn3.reshape(bd_, H_FOX * td, 1), (bd_, H_FOX * td, LANES))
        cnk = jnp.broadcast_to(cn3[:, :, None, :], (bd_, H_FOX, td, td)).reshape(bd_, H_FOX * td, td)
        cnk = jnp.pad(cnk, ((0, 0), (0, 0), (0, LANES - td)))
        pad_rows = lambda z: jnp.pad(z.reshape(bd_, td, FOX_W), ((0, 0), (0, page - td), (0, 0))).astype(BF16).transpose(0, 2, 1)
        pidx = (l * n_pool + page_table).reshape(-1).astype(jnp.int32)
        o_fox = _fox_decode(pidx, qbd, cnq, cnk, pad_rows(kn), pad_rows(fv), kc, vc, r3, s3, n_pages, td)
        o_fox = o_fox.reshape(bd_ * td, FOX_W)

        init = jnp.pad(state_gdn_conv[l].astype(F32), ((0, 0), (8 - (CONV_W - 1), 0), (0, 0)))
        gq, gk, gv, gg = _gdn_prep(p, ps, init, gdn_conv_w[l], gdn_A_log[l], gdn_dt_bias[l], bd_, td)
        padc = lambda z: jnp.pad(z.reshape(bd_, td, -1), ((0, 0), (0, c - td), (0, 0))).reshape(bd_ * c, -1)
        o_gdn, s_new = _gdn(padc(gq), padc(gk), padc(gv), padc(gg), state_gdn[l].astype(F32), bd_, c)
        o_gdn = o_gdn.reshape(bd_, c, GDN_W)[:, :td].reshape(bd_ * td, GDN_W)
        o_mem = _mem_attn(p, cache_mem_k[l].reshape(bd_, n_mem, MEM_W), cache_mem_v[l].reshape(bd_, n_mem, MEM_W),
                          mq_gain, bd_, td)
        conv_tail = jnp.concatenate([state_gdn_conv[l].astype(F32),
                                     p.reshape(bd_, td, -1)[:, :, GQ * FOX_W:GQ * FOX_W + 3 * GDN_W]],
                                    axis=1)[:, td:]
        ys = _out_proj(o_fox, o_gdn, o_mem, p, ys, wf, wg, wm, wo, gdn_gain)
        for lst, val in zip(outs[7:], (kn.reshape(bd_, td, H_FOX, DH_FOX), fv.reshape(bd_, td, H_FOX, DH_FOX),
                                       lf.reshape(bd_, td, LANES)[:, :, :H_FOX], s_new, conv_tail)):
            lst.append(val)

    return (yp.reshape(b, t, d), ys.reshape(bd_, td, d)) + tuple(jnp.stack(z) for z in outs)
```

```python
import functools

import numpy as np
import jax
import jax.numpy as jnp
from jax import lax
from jax.experimental import pallas as pl
from jax.experimental.pallas import tpu as pltpu

F32, BF16 = jnp.float32, jnp.bfloat16

H_FOX, DH_FOX = 8, 64
FOX_W = H_FOX * DH_FOX
H_GDN, DK_GDN, DV_GDN = 4, 128, 128
GDN_W = H_GDN * DV_GDN
CONV_W = 4
GDN_CHUNK = 64
H_MEM, DH_MEM = 4, 128
MEM_W = H_MEM * DH_MEM
N_BRANCH = 3
EPS = 1e-6
NEG = -1e30
LANES = 128
VMEM_LIMIT = 56 * 1024 * 1024
DECODE_PAGES_PER_STEP = 16

FQ, FK, FV, GQ, GK, GV, MQ, FZ = range(8)
GZ, MZ = 0, 1
GATE0 = 1
P_HALF = 8 * FOX_W
LOG2E = 1.4426950408889634
SM_FF, SM_GA, SM_GB = 0, 8, 12


def _params(*sem):
    return pltpu.CompilerParams(dimension_semantics=sem, vmem_limit_bytes=VMEM_LIMIT)


def _dot(a, b):
    return jnp.dot(a, b, preferred_element_type=F32)


def _dot_nt(a, b):
    return lax.dot_general(a, b, (((1,), (1,)), ((), ())), preferred_element_type=F32)


def _dot_tn(a, b):
    return lax.dot_general(a, b, (((0,), (0,)), ((), ())), preferred_element_type=F32)


def _split2(x):
    hi = x.astype(BF16)
    lo = (x - hi.astype(F32)).astype(BF16)
    return hi, lo


def _split3(x):
    hi = x.astype(BF16)
    r = x - hi.astype(F32)
    mid = r.astype(BF16)
    lo = (r - mid.astype(F32)).astype(BF16)
    return hi, mid, lo


def _dot3_lhs_exact(m, x):
    hi, mid, lo = _split3(x)
    return _dot(m, hi) + _dot(m, mid) + _dot(m, lo)


def _dot3_rhs_exact(x, m):
    hi, mid, lo = _split3(x)
    return _dot(hi, m) + _dot(mid, m) + _dot(lo, m)


def _sigmoid(x):
    return 1.0 / (1.0 + jnp.exp(-x))


def _silu(x):
    return x * _sigmoid(x)


def _softplus(x):
    return jnp.maximum(x, 0.0) + jnp.log1p(jnp.exp(-jnp.abs(x)))


def _log_sigmoid(x):
    return jnp.minimum(x, 0.0) - jnp.log1p(jnp.exp(-jnp.abs(x)))


def _lane_bcast(x, lane):
    return jnp.broadcast_to(x[:, lane:lane + 1], x.shape)


def _headnorm128(x, gain, nheads):
    outs = []
    for h in range(nheads):
        seg = x[:, h * LANES:(h + 1) * LANES]
        ms = jnp.mean(seg * seg, axis=-1, keepdims=True)
        outs.append(seg * lax.rsqrt(ms + EPS) * gain)
    return jnp.concatenate(outs, axis=1)


def _in_proj_kernel(x_ref, g_ref, w_ref, ws_ref, p32_ref, p16_ref, ps_ref, h_ref, *, n32):
    j = pl.program_id(1)

    @pl.when(j == 0)
    def _():
        x = x_ref[...]
        ms = jnp.mean(x * x, axis=-1, keepdims=True)
        h = (x * lax.rsqrt(ms + EPS) * g_ref[...]).astype(BF16)
        h_ref[...] = h
        ps_ref[...] = _dot(h, ws_ref[...])

    res = _dot(h_ref[...], w_ref[...])

    @pl.when(j < n32)
    def _():
        p32_ref[...] = res

    @pl.when(j >= n32)
    def _():
        p16_ref[...] = res.astype(BF16)


def _in_proj(x, gain, w_big, w_small, tm):
    n, d = x.shape
    tn = 1024
    n32 = P_HALF // tn
    n16 = (w_big.shape[1] - P_HALF) // tn
    return pl.pallas_call(
        functools.partial(_in_proj_kernel, n32=n32),
        grid=(n // tm, n32 + n16),
        in_specs=[pl.BlockSpec((tm, d), lambda i, j: (i, 0)),
                  pl.BlockSpec((1, d), lambda i, j: (0, 0)),
                  pl.BlockSpec((d, tn), lambda i, j: (0, j)),
                  pl.BlockSpec((d, LANES), lambda i, j: (0, 0))],
        out_specs=[pl.BlockSpec((tm, tn), lambda i, j: (i, jnp.minimum(j, n32 - 1))),
                   pl.BlockSpec((tm, tn), lambda i, j: (i, jnp.maximum(j - n32, 0))),
                   pl.BlockSpec((tm, LANES), lambda i, j: (i, 0))],
        out_shape=[jax.ShapeDtypeStruct((n, P_HALF), F32), jax.ShapeDtypeStruct((n, n16 * tn), BF16),
                   jax.ShapeDtypeStruct((n, LANES), F32)],
        scratch_shapes=[pltpu.VMEM((tm, d), BF16)],
        compiler_params=_params("parallel", "arbitrary"),
        name="in_proj")(x, gain, w_big, w_small)


def _fox_headnorm(x, gain, bd):
    hi, lo = _split2(x * x)
    ss = _dot(hi, bd) + _dot(lo, bd)
    return x * lax.rsqrt(ss * (1.0 / DH_FOX) + EPS) * gain


def _fox_prep_kernel(fq_ref, fk_ref, fv_ref, ps_ref, qg_ref, kg_ref, fb_ref, bd_ref, tri_ref,
                     selq_ref, selk_ref, selv_ref, qrow_ref, krow_ref,
                     kn_ref, v_ref, lf_ref, qa_ref, ka_ref, va_ref, carry_ref):
    @pl.when(pl.program_id(1) == 0)
    def _():
        carry_ref[...] = jnp.zeros_like(carry_ref)

    bd = bd_ref[...]
    qn = _fox_headnorm(fq_ref[...], qg_ref[...], bd) * (DH_FOX ** -0.5 * LOG2E)
    kn = _fox_headnorm(fk_ref[...], kg_ref[...], bd)
    fv = fv_ref[...]
    kn_ref[0] = kn
    v_ref[0] = fv
    lf = _log_sigmoid(ps_ref[...] + fb_ref[...])
    lf_ref[0] = lf
    c = _dot3_lhs_exact(tri_ref[...], lf) + carry_ref[0:1, :]
    tm = c.shape[0]
    carry_ref[...] = jnp.broadcast_to(c[tm - 1:tm, :], carry_ref.shape)
    chi, cmid, clo = _split3(c * LOG2E)
    qa =_dot(jnp.concatenate([qn.astype(BF16), chi, cmid, clo], axis=1), selq_ref[...]) + qrow_ref[...]
    ka = _dot(jnp.concatenate([kn.astype(BF16), chi, cmid, clo], axis=1), selk_ref[...]) + krow_ref[...]
    qa_ref[0] = qa.astype(BF16)
    ka_ref[0] = ka.astype(BF16)
    vt = _dot_nt(selv_ref[...], fv.astype(BF16))
    r = lax.broadcasted_iota(jnp.int32, vt.shape, 0) & (2 * LANES - 1)
    va_ref[0, 0] = jnp.where((r == DH_FOX) | (r == LANES), 1.0, vt).astype(BF16)


def _fox_consts(tm):
    a = np.arange(FOX_W)
    bd = (a[:, None] // DH_FOX == a[None, :] // DH_FOX).astype(np.float32)
    r = np.arange(tm)
    tri = (r[None, :] <= r[:, None]).astype(np.float32)
    aw = H_FOX * LANES
    selq = np.zeros((FOX_W + 3 * LANES, aw), np.float32)
    selk = np.zeros((FOX_W + 3 * LANES, aw), np.float32)
    selv = np.zeros((aw, FOX_W), np.float32)
    qrow = np.zeros((1, aw), np.float32)
    krow = np.zeros((1, aw), np.float32)
    for h in range(H_FOX):
        for d in range(DH_FOX):
            selq[h * DH_FOX + d, h * LANES + d] = 1.0
            selk[h * DH_FOX + d, h * LANES + d] = 1.0
            selv[h * LANES + (h % 2) * DH_FOX + d, h * DH_FOX + d] = 1.0
        for e in range(3):
            selq[FOX_W + e * LANES + h, h * LANES + DH_FOX + e] = 1.0
            selk[FOX_W + e * LANES + h, h * LANES + DH_FOX + 3 + e] = -1.0
            qrow[0, h * LANES + DH_FOX + 3 + e] = 1.0
            krow[0, h * LANES + DH_FOX + e] = 1.0
    bf = lambda z: jnp.asarray(z, BF16)
    return bf(bd), bf(tri), bf(selq), bf(selk), bf(selv), jnp.asarray(qrow), jnp.asarray(krow)


def _tile_gain(g, reps):
    return jnp.tile(g.astype(F32), reps)[None, :]


def _pad_row(vals, offset):
    row = jnp.zeros((LANES,), F32)
    return lax.dynamic_update_slice(row, vals.astype(F32), (offset,))[None, :]


def _fox_prep_prompt(p, ps, qgain, kgain, fbias, b, t):
    tm = min(512, t)
    nt = t // tm
    bd, tri, selq, selk, selv, qrow, krow = _fox_consts(tm)
    aw = H_FOX * LANES
    row = lambda blk: pl.BlockSpec((tm, FOX_W), lambda i, j, blk=blk: (i * nt + j, blk))
    const = lambda shape: pl.BlockSpec(shape, lambda i, j: (0,) * len(shape))
    tok = lambda w: pl.BlockSpec((1, tm, w), lambda i, j: (i, j, 0))
    return pl.pallas_call(
        _fox_prep_kernel,
        grid=(b, nt),
        in_specs=[row(FQ), row(FK), row(FV),
                  pl.BlockSpec((tm, LANES), lambda i, j: (i * nt + j, 0)),
                  const((1, FOX_W)), const((1, FOX_W)), const((1, LANES)),
                  const(bd.shape), const(tri.shape), const(selq.shape), const(selk.shape), const(selv.shape),
                  const(qrow.shape), const(krow.shape)],
        out_specs=[tok(FOX_W), tok(FOX_W), tok(LANES), tok(aw), tok(aw),
                   pl.BlockSpec((1, 1, aw, tm), lambda i, j: (i, j, 0, 0))],
        out_shape=[jax.ShapeDtypeStruct((b, t, FOX_W), F32), jax.ShapeDtypeStruct((b, t, FOX_W), F32),
                   jax.ShapeDtypeStruct((b, t, LANES), F32),
                   jax.ShapeDtypeStruct((b, t, aw), BF16), jax.ShapeDtypeStruct((b, t, aw), BF16),
                   jax.ShapeDtypeStruct((b, nt, aw, tm), BF16)],
        scratch_shapes=[pltpu.VMEM((8, LANES), F32)],
        compiler_params=_params("parallel", "arbitrary"),
        name="fox_prep")(p, p, p, ps, _tile_gain(qgain, H_FOX), _tile_gain(kgain, H_FOX), _pad_row(fbias, SM_FF),
                         bd, tri, selq, selk, selv, qrow, krow)


def _fox_attn_kernel(q_ref, k_ref, vt_ref, o_ref, m_ref, acc_ref, sa_ref, sb_ref, *, tk):
    qi = pl.program_id(2)
    tq = q_ref.shape[1]
    heads = [slice(hh * LANES, (hh + 1) * LANES) for hh in range(2)]
    m_ref[...] = jnp.full(m_ref.shape, NEG, F32)
    acc_ref[...] = jnp.zeros_like(acc_ref)

    def scores(ki, hh):
        r0 = pl.multiple_of(ki * tk, tk)
        return _dot_nt(k_ref[0, pl.ds(r0, tk), heads[hh]], q_ref[0, :, heads[hh]])

    def accumulate(st, ki, hh):
        m_prev = m_ref[hh]
        m_new = jnp.maximum(m_prev, jnp.max(st, axis=0, keepdims=True))
        pt = jnp.exp2(st - m_new).astype(BF16)
        acc_ref[hh] = jnp.exp2(m_prev - m_new) * acc_ref[hh] + _dot(vt_ref[0, ki, heads[hh], :], pt)
        m_ref[hh] = m_new

    def causal(st):
        krow = lax.broadcasted_iota(jnp.int32, (tk, tq), 0)
        qcol = lax.broadcasted_iota(jnp.int32, (tk, tq), 1)
        return jnp.where(krow <= qcol, st, NEG)

    for hh in range(2):
        sa_ref[hh] = scores(0, hh)

    def pair(j, carry):
        for hh in range(2):
            sb_ref[hh] = scores(2 * j + 1, hh)
            accumulate(sa_ref[hh], 2 * j, hh)
        for hh in range(2):
            sa_ref[hh] = scores(2 * j + 2, hh)
            accumulate(sb_ref[hh], 2 * j + 1, hh)
        return carry

    lax.fori_loop(0, qi // 2, pair, 0)

    @pl.when(qi % 2 == 1)
    def _():
        for hh in range(2):
            sb_ref[hh] = scores(qi, hh)
            accumulate(sa_ref[hh], qi - 1, hh)
        for hh in range(2):
            accumulate(causal(sb_ref[hh]), qi, hh)

    @pl.when(qi % 2 == 0)
    def _():
        for hh in range(2):
            accumulate(causal(sa_ref[hh]), qi, hh)

    a0, a1 = acc_ref[0], acc_ref[1]
    row = lax.broadcasted_iota(jnp.int32, a0.shape, 0)
    ot = jnp.where(row < DH_FOX, a0 / a0[DH_FOX:DH_FOX + 1, :], a1 / a1[0:1, :])
    o_ref[0] = ot.T


def _fox_attn_prompt(qa, ka, vat, b, t):
    tq = vat.shape[3]
    pair = 2 * LANES
    return pl.pallas_call(
        functools.partial(_fox_attn_kernel, tk=tq),
        grid=(b, H_FOX // 2, t // tq),
        in_specs=[pl.BlockSpec((1, tq, pair), lambda i, h, q: (i, q, h)),
                  pl.BlockSpec((1, t, pair), lambda i, h, q: (i, 0, h)),
                  pl.BlockSpec((1, t // tq, pair, tq), lambda i, h, q: (i, 0, h, 0))],
        out_specs=pl.BlockSpec((1, tq, LANES), lambda i, h, q: (i, q, h)),
        out_shape=jax.ShapeDtypeStruct((b, t, FOX_W), F32),
        scratch_shapes=[pltpu.VMEM((2, 1, tq), F32), pltpu.VMEM((2, LANES, tq), F32),
                        pltpu.VMEM((2, tq, tq), F32), pltpu.VMEM((2, tq, tq), F32)],
        compiler_params=_params("parallel", "parallel", "arbitrary"),
        name="fox_attn")(qa, ka, vat)


def _fox_sample_prep_kernel(fq_ref, fk_ref, ps_ref, qg_ref, kg_ref, fb_ref, bd_ref, tri_ref,
                            qn_ref, kn_ref, lf_ref, cn_ref):
    bd = bd_ref[...]
    qn_ref[...] = _fox_headnorm(fq_ref[...], qg_ref[...], bd) * (DH_FOX ** -0.5)
    kn_ref[...] = _fox_headnorm(fk_ref[...], kg_ref[...], bd)
    lf = _log_sigmoid(ps_ref[...] + fb_ref[...])
    lf_ref[...] = lf
    cn_ref[...] = _dot3_lhs_exact(tri_ref[...], lf)


def _fox_sample_prep(p, ps, qgain, kgain, fbias, td):
    n = p.shape[0]
    a = np.arange(FOX_W)
    bd = jnp.asarray((a[:, None] // DH_FOX == a[None, :] // DH_FOX), BF16)
    r = np.arange(n)
    tri = jnp.asarray((r[None, :] <= r[:, None]) & (r[None, :] // td == r[:, None] // td), BF16)
    full = lambda shape: pl.BlockSpec(shape, lambda i: (0,) * len(shape))
    return pl.pallas_call(
        _fox_sample_prep_kernel,
        grid=(1,),
        in_specs=[pl.BlockSpec((n, FOX_W), lambda i: (0, FQ)), pl.BlockSpec((n, FOX_W), lambda i: (0, FK)),
                  full((n, LANES)), full((1, FOX_W)), full((1, FOX_W)), full((1, LANES)),
                  full(bd.shape), full(tri.shape)],
        out_specs=[full((n, FOX_W)), full((n, FOX_W)), full((n, LANES)), full((n, LANES))],
        out_shape=[jax.ShapeDtypeStruct((n, FOX_W), F32), jax.ShapeDtypeStruct((n, FOX_W), F32),
                   jax.ShapeDtypeStruct((n, LANES), F32), jax.ShapeDtypeStruct((n, LANES), F32)],
        compiler_params=_params("arbitrary"),
        name="fox_sample_prep")(p, p, ps, _tile_gain(qgain, H_FOX), _tile_gain(kgain, H_FOX),
                                _pad_row(fbias, SM_FF), bd, tri)


def _page_logf_kernel(x_ref, u_ref, ones_ref, r_ref, s_ref):
    x = x_ref[...]
    r_ref[...] = _dot3_rhs_exact(x, u_ref[...])
    s_ref[...] = _dot3_rhs_exact(x, ones_ref[...])


def _page_logf(xt):
    n, page = xt.shape
    tm = 2048
    while n % tm:
        tm //= 2
    j = np.arange(page)
    u = jnp.asarray(j[:, None] > j[None, :], BF16)
    ones = jnp.ones((page, page), BF16)
    return pl.pallas_call(
        _page_logf_kernel,
        grid=(n // tm,),
        in_specs=[pl.BlockSpec((tm, page), lambda i: (i, 0)),
                  pl.BlockSpec((page, page), lambda i: (0, 0)), pl.BlockSpec((page, page), lambda i: (0, 0))],
        out_specs=[pl.BlockSpec((tm, page), lambda i: (i, 0)), pl.BlockSpec((tm, page), lambda i: (i, 0))],
        out_shape=[jax.ShapeDtypeStruct((n, page), F32), jax.ShapeDtypeStruct((n, page), F32)],
        compiler_params=_params("parallel"),
        name="page_logf")(xt, u, ones)


def _fox_decode_kernel(pidx_ref, q_ref, cn_ref, cnk_ref, knew_ref, vnew_ref, *rest, g, td):
    k_refs, v_refs, r_refs, s_refs = rest[0:g], rest[g:2 * g], rest[2 * g:3 * g], rest[3 * g:4 * g]
    o_ref, m_ref, l_ref, acc_ref, suf_ref = rest[4 * g:]
    p = pl.program_id(1)
    q = q_ref[0]
    cn = cn_ref[0]
    nrow = q.shape[0]

    @pl.when(p == 0)
    def _():
        s = _dot(q, knew_ref[0])
        qpos = lax.broadcasted_iota(jnp.int32, s.shape, 0) % td
        kpos = lax.broadcasted_iota(jnp.int32, s.shape, 1)
        s = jnp.where(kpos <= qpos, s + cn - cnk_ref[0], NEG)
        m = jnp.max(s, axis=-1, keepdims=True)
        pr = jnp.exp(s - m)
        m_ref[...] = m
        l_ref[...] = jnp.sum(pr, axis=-1, keepdims=True)
        acc_ref[...] = _dot_nt(pr.astype(BF16), vnew_ref[0])
        suf_ref[...] = jnp.zeros_like(suf_ref)

    suf = suf_ref[...]
    scores = []
    for j in range(g):
        z = r_refs[j][0] + suf
        bias = jnp.concatenate([jnp.broadcast_to(z[h:h + 1, :], (td, LANES)) for h in range(H_FOX)], axis=0)
        scores.append(_dot(q, k_refs[j][0].astype(BF16)) + bias + cn)
        suf = suf + s_refs[j][0]
    suf_ref[...] = suf
    m_prev = m_ref[...]
    m_new = m_prev
    for s in scores:
        m_new = jnp.maximum(m_new, jnp.max(s, axis=-1, keepdims=True))
    alpha = jnp.exp(m_prev - m_new)
    l = alpha * l_ref[...]
    acc = alpha * acc_ref[...]
    for j in range(g):
        pr = jnp.exp(scores[j] - m_new)
        l = l + jnp.sum(pr, axis=-1, keepdims=True)
        acc = acc + _dot_nt(pr.astype(BF16), v_refs[j][0].astype(BF16))
    m_ref[...] = m_new
    l_ref[...] = l
    acc_ref[...] = acc

    @pl.when(p == pl.num_programs(1) - 1)
    def _():
        o = acc / l
        head = lax.broadcasted_iota(jnp.int32, (td, FOX_W), 1) // DH_FOX
        out = jnp.zeros((td, FOX_W), F32)
        for h in range(H_FOX):
            out = out + jnp.where(head == h, o[h * td:(h + 1) * td, :], 0.0)
        o_ref[0] = out


def _fox_decode(pidx, qbd, cn, cnk, knew, vnew, kc, vc, r3, s3, n_pages, td):
    bd_, nrow, _ = qbd.shape
    page = kc.shape[2]
    g = min(DECODE_PAGES_PER_STEP, n_pages)
    steps = n_pages // g

    def page_map(j):
        return lambda b, p, idx: (idx[b * n_pages + (n_pages - 1 - (p * g + j))], 0, 0)

    per_b = lambda shape: pl.BlockSpec((1,) + shape, lambda b, p, idx: (b, 0, 0))
    in_specs = [per_b((nrow, FOX_W)), per_b((nrow, LANES)), per_b((nrow, LANES)),
                per_b((FOX_W, page)), per_b((FOX_W, page))]
    in_specs += [pl.BlockSpec((1, FOX_W, page), page_map(j)) for j in range(g)]
    in_specs += [pl.BlockSpec((1, FOX_W, page), page_map(j)) for j in range(g)]
    in_specs += [pl.BlockSpec((1, H_FOX, LANES), page_map(j)) for j in range(g)]
    in_specs += [pl.BlockSpec((1, H_FOX, LANES), page_map(j)) for j in range(g)]
    grid_spec = pltpu.PrefetchScalarGridSpec(
        num_scalar_prefetch=1, grid=(bd_, steps), in_specs=in_specs,
        out_specs=pl.BlockSpec((1, td, FOX_W), lambda b, p, idx: (b, 0, 0)),
        scratch_shapes=[pltpu.VMEM((nrow, 1), F32), pltpu.VMEM((nrow, 1), F32), pltpu.VMEM((nrow, FOX_W), F32),
                        pltpu.VMEM((H_FOX, LANES), F32)])
    return pl.pallas_call(
        functools.partial(_fox_decode_kernel, g=g, td=td),
        grid_spec=grid_spec,
        out_shape=jax.ShapeDtypeStruct((bd_, td, FOX_W), F32),
        compiler_params=_params("parallel", "arbitrary"),
        name="fox_decode")(pidx, qbd, cn, cnk, knew, vnew, *([kc] * g), *([vc] * g), *([r3] * g), *([s3] * g))


def _gdn_prep_kernel(gq_ref, gk_ref, gv_ref, ps_ref, init_ref, w_ref, alog_ref, dtb_ref,
                     q_ref, k_ref, v_ref, g_ref, ext_ref):
    tm = gq_ref.shape[0]

    @pl.when(pl.program_id(1) == 0)
    def _():
        ext_ref[0:8, :] = init_ref[0]

    @pl.when(pl.program_id(1) > 0)
    def _():
        ext_ref[0:8, :] = ext_ref[tm:tm + 8, :]

    ext_ref[8:8 + tm, 0:GDN_W] = gq_ref[...]
    ext_ref[8:8 + tm, GDN_W:2 * GDN_W] = gk_ref[...]
    ext_ref[8:8 + tm, 2 * GDN_W:3 * GDN_W] = gv_ref[...]
    conv = ext_ref[8:8 + tm, :] * w_ref[CONV_W - 1:CONV_W, :]
    for i in range(CONV_W - 1):
        off = 8 - (CONV_W - 1) + i
        conv = conv + ext_ref[off:off + tm, :] * w_ref[i:i + 1, :]
    s = _silu(conv)

    def l2(x):
        outs = []
        for h in range(H_GDN):
            seg = x[:, h * LANES:(h + 1) * LANES]
            outs.append(seg * lax.rsqrt(jnp.sum(seg * seg, axis=-1, keepdims=True) + EPS))
        return jnp.concatenate(outs, axis=1)

    q_ref[...] = l2(s[:, 0:GDN_W]) * (DK_GDN ** -0.5)
    k_ref[...] = l2(s[:, GDN_W:2 * GDN_W])
    v_ref[...] = s[:, 2 * GDN_W:3 * GDN_W]
    ps = ps_ref[...]
    lane = lax.broadcasted_iota(jnp.int32, ps.shape, 1)
    gate = -jnp.exp(alog_ref[...]) * _softplus(ps + dtb_ref[...])
    beta = _sigmoid(ps)
    g_ref[...] = jnp.where((lane >= SM_GA) & (lane < SM_GB), gate,
                           jnp.where((lane >= SM_GB) & (lane < SM_GB + H_GDN), beta, 0.0))


def _gdn_prep(p, ps, init, conv_w, a_log, dt_bias, b, t):
    tm = min(512, t)
    nt = t // tm
    n = b * t
    row = lambda blk: pl.BlockSpec((tm, GDN_W), lambda i, j, blk=blk: (i * nt + j, blk))
    out = pl.BlockSpec((tm, GDN_W), lambda i, j: (i * nt + j, 0))
    small = pl.BlockSpec((tm, LANES), lambda i, j: (i * nt + j, 0))
    const = lambda shape: pl.BlockSpec(shape, lambda i, j: (0,) * len(shape))
    return pl.pallas_call(
        _gdn_prep_kernel,
        grid=(b, nt),
        in_specs=[row(GQ), row(GK), row(GV), small,
                  pl.BlockSpec((1, 8, 3 * GDN_W), lambda i, j: (i, 0, 0)),
                  const((CONV_W, 3 * GDN_W)), const((1, LANES)), const((1, LANES))],
        out_specs=[out, out, out, small],
        out_shape=[jax.ShapeDtypeStruct((n, GDN_W), F32)] * 3 + [jax.ShapeDtypeStruct((n, LANES), F32)],
        scratch_shapes=[pltpu.VMEM((tm + 8, 3 * GDN_W), F32)],
        compiler_params=_params("parallel", "arbitrary"),
        name="gdn_prep")(p, p, p, ps, init, conv_w.astype(F32), _pad_row(a_log, SM_GA), _pad_row(dt_bias, SM_GA))


def _gdn_chunk_kernel(q_ref, k_ref, v_ref, g_ref, tri_ref,
                      m_ref, qk_ref, u_ref, w_ref, qg_ref, kg_ref, egl_ref, *, nch):
    c = GDN_CHUNK
    tri = tri_ref[...]
    ri = lax.broadcasted_iota(jnp.int32, (c, c), 0)
    ci = lax.broadcasted_iota(jnp.int32, (c, c), 1)
    lane = lax.broadcasted_iota(jnp.int32, (c, LANES), 1)
    for cc in range(nch):
        rows = slice(cc * c, (cc + 1) * c)
        gt = g_ref[rows, :]
        gcum = _dot3_lhs_exact(tri, gt)
        for h in range(H_GDN):
            cols = slice(h * LANES, (h + 1) * LANES)
            gcol = _lane_bcast(gcum, SM_GA + h)
            beta = _lane_bcast(gt, SM_GB + h)
            hi, mid, lo = [z.astype(F32) for z in _split3(gcol)]
            a = jnp.where(lane == 0, hi, jnp.where(lane == 1, mid, jnp.where(lane == 2, lo,
                          jnp.where(lane < 6, 1.0, 0.0))))
            bm = jnp.where(lane < 3, 1.0, jnp.where(lane == 3, -hi, jnp.where(lane == 4, -mid,
                           jnp.where(lane == 5, -lo, 0.0))))
            diff = _dot_nt(a.astype(BF16), bm.astype(BF16))
            decay = jnp.exp(jnp.where(ci <= ri, diff, NEG))
            qh, kh, vh = q_ref[rows, cols], k_ref[rows, cols], v_ref[rows, cols]
            kb = kh * beta
            khb = kh.astype(BF16)
            m_ref[cc * H_GDN + h] = jnp.where(ci < ri, _dot_nt(kb.astype(BF16), khb) * decay, 0.0)
            qk_ref[cc * H_GDN + h] = _dot_nt(qh.astype(BF16), khb) * decay
            eg = jnp.exp(gcol)
            glast = gcol[c - 1:c, :]
            u_ref[rows, cols] = vh * beta
            w_ref[rows, cols] = kb * eg
            qg_ref[rows, cols] = qh * eg
            kg_ref[rows, cols] = kh * jnp.exp(glast - gcol)
            egl_ref[cc * H_GDN + h] = jnp.broadcast_to(jnp.exp(glast), (8, LANES))


def _gdn_chunk(q, k, v, g):
    n = q.shape[0]
    c = GDN_CHUNK
    tm = min(256, n)
    nch = tm // c
    r = np.arange(c)
    tri = jnp.asarray(r[None, :] <= r[:, None], BF16)
    nc = n // c * H_GDN
    row = pl.BlockSpec((tm, GDN_W), lambda i: (i, 0))
    blk = lambda shape: pl.BlockSpec((nch * H_GDN,) + shape, lambda i: (i, 0, 0))
    return pl.pallas_call(
        functools.partial(_gdn_chunk_kernel, nch=nch),
        grid=(n // tm,),
        in_specs=[row, row, row, pl.BlockSpec((tm, LANES), lambda i: (i, 0)), pl.BlockSpec((c, c), lambda i: (0, 0))],
        out_specs=[blk((c, c)), blk((c, c)), row, row, row, row, blk((8, LANES))],
        out_shape=[jax.ShapeDtypeStruct((nc, c, c), F32), jax.ShapeDtypeStruct((nc, c, c), F32)]
                  + [jax.ShapeDtypeStruct((n, GDN_W), F32)] * 4 + [jax.ShapeDtypeStruct((nc, 8, LANES), F32)],
        compiler_params=_params("parallel"),
        name="gdn_chunk")(q, k, v, g, tri)


def _gdn_inv_kernel(m_ref, x_ref):
    c = GDN_CHUNK
    for i in range(c):
        ext = 8 * (i // 8 + 1)
        e_i = (lax.broadcasted_iota(jnp.int32, (ext, LANES), 0) == i).astype(F32)

        def body(j, acc, i=i, ext=ext):
            return acc - m_ref[i, pl.ds(j, 1), :] * x_ref[j, 0:ext, :]

        x_ref[i, 0:ext, :] = lax.fori_loop(0, i, body, e_i, unroll=min(max(i, 1), 8))
        if ext < c:
            x_ref[i, ext:c, :] = jnp.zeros((c - ext, LANES), F32)


def _gdn_inv(mt):
    c = GDN_CHUNK
    nc = mt.shape[2]
    spec = pl.BlockSpec((c, c, LANES), lambda i: (0, 0, i))
    return pl.pallas_call(
        _gdn_inv_kernel,
        grid=(nc // LANES,),
        in_specs=[spec], out_specs=spec,
        out_shape=jax.ShapeDtypeStruct((c, c, nc), F32),
        compiler_params=_params("parallel"),
        name="gdn_inv")(mt)


def _gdn_scan_kernel(x_ref, qk_ref, u_ref, w_ref, qg_ref, kg_ref, egl_ref, s0_ref, o_ref, sout_ref,
                     s_ref, us_ref, wq_ref, *, nch):
    c = GDN_CHUNK

    @pl.when(pl.program_id(1) == 0)
    def _():
        s_ref[...] = s0_ref[0]

    heads = [slice(h * LANES, (h + 1) * LANES) for h in range(H_GDN)]

    def rows(cc):
        return cc * c if isinstance(cc, int) else pl.multiple_of(cc * c, c)

    def solve(cc, slot):
        r0 = rows(cc)
        for h in range(H_GDN):
            rhs = jnp.concatenate([u_ref[pl.ds(r0, c), heads[h]], w_ref[pl.ds(r0, c), heads[h]]], axis=1)
            xh, xl = _split2(x_ref[cc * H_GDN + h])
            rh, rl = _split2(rhs)
            sol = _dot(xh, rh) + _dot(xh, rl) + _dot(xl, rh)
            us_ref[slot, :, heads[h]] = sol[:, 0:DV_GDN]
            wq_ref[slot, 0:c, heads[h]] = sol[:, DV_GDN:].astype(BF16)
            wq_ref[slot, c:2 * c, heads[h]] = qg_ref[pl.ds(r0, c), heads[h]].astype(BF16)

    def advance(cc, slot):
        r0 = rows(cc)
        sts = [s_ref[h] for h in range(H_GDN)]
        boths = [_dot(wq_ref[slot, :, heads[h]], sts[h].astype(BF16)) for h in range(H_GDN)]
        dbs = [(us_ref[slot, :, heads[h]] - boths[h][0:c]).astype(BF16) for h in range(H_GDN)]
        for h in range(H_GDN):
            egl = jnp.broadcast_to(egl_ref[cc * H_GDN + h][0:1, :], (DK_GDN, DV_GDN))
            s_ref[h] = sts[h] * egl + _dot_tn(kg_ref[pl.ds(r0, c), heads[h]].astype(BF16), dbs[h])
        for h in range(H_GDN):
            o_ref[pl.ds(r0, c), heads[h]] = boths[h][c:] + _dot(qk_ref[cc * H_GDN + h].astype(BF16), dbs[h])

    solve(0, 0)

    def pair(j, carry):
        solve(2 * j + 1, 1)
        advance(2 * j, 0)
        solve(jnp.minimum(2 * j + 2, nch - 1), 0)
        advance(2 * j + 1, 1)
        return carry

    lax.fori_loop(0, nch // 2, pair, 0)
    if nch % 2:
        advance(nch - 1, 0)

    @pl.when(pl.program_id(1) == pl.num_programs(1) - 1)
    def _():
        sout_ref[0] = s_ref[...]


def _gdn_scan(x, qk, u, w, qg, kg, egl, s0, b, t):
    c = GDN_CHUNK
    tg = min(1024, t)
    ng = t // tg
    nch = tg // c
    row = pl.BlockSpec((tg, GDN_W), lambda i, j: (i * ng + j, 0))
    blk = lambda shape: pl.BlockSpec((nch * H_GDN,) + shape, lambda i, j: (i * ng + j, 0, 0))
    st = pl.BlockSpec((1, H_GDN, DK_GDN, DV_GDN), lambda i, j: (i, 0, 0, 0))
    return pl.pallas_call(
        functools.partial(_gdn_scan_kernel, nch=nch),
        grid=(b, ng),
        in_specs=[blk((c, c)), blk((c, c)), row, row, row, row, blk((8, LANES)), st],
        out_specs=[row, st],
        out_shape=[jax.ShapeDtypeStruct((b * t, GDN_W), F32), jax.ShapeDtypeStruct((b, H_GDN, DK_GDN, DV_GDN), F32)],
        scratch_shapes=[pltpu.VMEM((H_GDN, DK_GDN, DV_GDN), F32), pltpu.VMEM((2, c, GDN_W), F32),
                        pltpu.VMEM((2, 2 * c, GDN_W), BF16)],
        compiler_params=_params("parallel", "arbitrary"),
        name="gdn_scan")(x, qk, u, w, qg, kg, egl, s0)


def _gdn(q, k, v, g, s0, b, t):
    c = GDN_CHUNK
    m, qk, u, w, qg, kg, egl = _gdn_chunk(q, k, v, g)
    nc = m.shape[0]
    pad = (-nc) % LANES
    mt = jnp.pad(m.reshape(nc, c * c), ((0, pad), (0, 0))).T.reshape(c, c, nc + pad)
    x = _gdn_inv(mt).reshape(c * c, nc + pad).T[:nc].reshape(nc, c, c)
    return _gdn_scan(x, qk, u, w, qg, kg, egl, s0, b, t)


def _mem_kv_kernel(m_ref, g_ref, w_ref, kg_ref, mk_ref, mv_ref):
    x = m_ref[...]
    ms = jnp.mean(x * x, axis=-1, keepdims=True)
    h = (x * lax.rsqrt(ms + EPS) * g_ref[...]).astype(BF16)
    kv = _dot(h, w_ref[...])
    mk_ref[...] = _headnorm128(kv[:, 0:MEM_W], kg_ref[...], H_MEM)
    mv_ref[...] = kv[:, MEM_W:]


def _mem_kv(mem, gain, w_kv, k_gain):
    n, d = mem.shape
    tm = 256
    return pl.pallas_call(
        _mem_kv_kernel,
        grid=(n // tm,),
        in_specs=[pl.BlockSpec((tm, d), lambda i: (i, 0)), pl.BlockSpec((1, d), lambda i: (0, 0)),
                  pl.BlockSpec((d, 2 * MEM_W), lambda i: (0, 0)), pl.BlockSpec((1, LANES), lambda i: (0, 0))],
        out_specs=[pl.BlockSpec((tm, MEM_W), lambda i: (i, 0))] * 2,
        out_shape=[jax.ShapeDtypeStruct((n, MEM_W), F32)] * 2,
        compiler_params=_params("parallel"),
        name="mem_kv")(mem, gain, w_kv, k_gain)


def _mem_attn_kernel(q_ref, mk_ref, mv_ref, g_ref, o_ref, *, cast):
    dt = BF16 if cast else F32
    qn = _headnorm128(q_ref[...], g_ref[...], H_MEM)
    outs = []
    for h in range(H_MEM):
        cols = slice(h * LANES, (h + 1) * LANES)
        s = _dot_nt(qn[:, cols].astype(dt), mk_ref[0, :, cols].astype(dt)) * (DH_MEM ** -0.5)
        p = jnp.exp(s - jnp.max(s, axis=-1, keepdims=True))
        p = p / jnp.sum(p, axis=-1, keepdims=True)
        outs.append(_dot(p.astype(dt), mv_ref[0, :, cols].astype(dt)))
    o_ref[...] = jnp.concatenate(outs, axis=1)


def _mem_attn(p, mk, mv, q_gain, b, t):
    tm = min(512, t)
    nt = t // tm
    n_mem = mk.shape[1]
    kv = pl.BlockSpec((1, n_mem, MEM_W), lambda i, j: (i, 0, 0))
    return pl.pallas_call(
        functools.partial(_mem_attn_kernel, cast=tm >= 16),
        grid=(b, nt),
        in_specs=[pl.BlockSpec((tm, MEM_W), lambda i, j: (i * nt + j, MQ)), kv, kv,
                  pl.BlockSpec((1, LANES), lambda i, j: (0, 0))],
        out_specs=pl.BlockSpec((tm, MEM_W), lambda i, j: (i * nt + j, 0)),
        out_shape=jax.ShapeDtypeStruct((b * t, MEM_W), F32),
        compiler_params=_params("parallel", "parallel"),
        name="mem_attn")(p, mk, mv, q_gain)


def _out_kernel(of_ref, og_ref, om_ref, fz_ref, gz_ref, mz_ref, g0_ref, g1_ref, g2_ref, x_ref,
                wf_ref, wg_ref, wm_ref, wo_ref, gg_ref, y_ref):
    f32 = lambda ref: ref[...].astype(F32)
    a = (of_ref[...] * _silu(fz_ref[...])).astype(BF16)
    b = (_headnorm128(og_ref[...], gg_ref[...], H_GDN) * _silu(f32(gz_ref))).astype(BF16)
    c = (om_ref[...] * _silu(f32(mz_ref))).astype(BF16)
    merged = (_sigmoid(f32(g0_ref)) * _dot(a, wf_ref[...]) + _sigmoid(f32(g1_ref)) * _dot(b, wg_ref[...])
              + _sigmoid(f32(g2_ref)) * _dot(c, wm_ref[...]))
    y_ref[...] = x_ref[...] + _dot(merged.astype(BF16), wo_ref[...])


def _out_proj(o_fox, o_gdn, o_mem, p32, p16, x, wf, wg, wm, wo, gdn_gain):
    n, d = x.shape
    tm = min(256, n)
    branch = pl.BlockSpec((tm, FOX_W), lambda i: (i, 0))
    pz = lambda blk: pl.BlockSpec((tm, FOX_W), lambda i, blk=blk: (i, blk))
    pg = lambda blk: pl.BlockSpec((tm, d), lambda i, blk=blk: (i, blk))
    const = lambda shape: pl.BlockSpec(shape, lambda i: (0, 0))
    return pl.pallas_call(
        _out_kernel,
        grid=(n // tm,),
        in_specs=[branch, branch, branch, pz(FZ), pz(GZ), pz(MZ), pg(GATE0), pg(GATE0 + 1), pg(GATE0 + 2),
                  pl.BlockSpec((tm, d), lambda i: (i, 0)),
                  const(wf.shape), const(wg.shape), const(wm.shape), const(wo.shape), const((1, LANES))],
        out_specs=pl.BlockSpec((tm, d), lambda i: (i, 0)),
        out_shape=jax.ShapeDtypeStruct((n, d), F32),
        compiler_params=_params("parallel"),
        name="out_proj")(o_fox, o_gdn, o_mem, p32, p16, p16, p16, p16, p16, x, wf, wg, wm, wo, gdn_gain)


def _split_w_in(w):
    o = np.cumsum([0, FOX_W, FOX_W, FOX_W, FOX_W, H_FOX, 3 * GDN_W, GDN_W, H_GDN, H_GDN, MEM_W, MEM_W])
    big = jnp.concatenate([w[:, o[0]:o[3]], w[:, o[5]:o[6]], w[:, o[9]:o[10]], w[:, o[3]:o[4]],
                           w[:, o[6]:o[7]], w[:, o[10]:o[11]], w[:, o[11]:]], axis=1).astype(BF16)
    assert big.shape[1] == 2 * P_HALF
    small = jnp.concatenate([w[:, o[4]:o[5]], w[:, o[7]:o[9]]], axis=1)
    small = jnp.pad(small, ((0, 0), (0, LANES - small.shape[1]))).astype(BF16)
    return big, small


def kernel(x_prompt, x_sample, cache_fox_k, cache_fox_v, cache_fox_logf, state_gdn, state_gdn_conv,
           cache_mem_k, cache_mem_v, page_table, mem_prompt, ln_gain, w_in, fox_q_gain, fox_k_gain,
           fox_f_bias, gdn_conv_w, gdn_A_log, gdn_dt_bias, gdn_out_gain, mem_norm_gain, w_mem_kv,
           mem_q_gain, mem_k_gain, w_fox_br, w_gdn_br, w_mem_br, w_out):
    b, t, d = x_prompt.shape
    bd_, td, _ = x_sample.shape
    depth, n_pool, page = cache_fox_k.shape[:3]
    n_pages = page_table.shape[1]
    n_mem = mem_prompt.shape[1]
    c = GDN_CHUNK
    assert page == LANES and t % c == 0 and td <= c and td % 8 == 0

    kc = jnp.transpose(cache_fox_k, (0, 1, 3, 4, 2)).reshape(depth * n_pool, FOX_W, page)
    vc = jnp.transpose(cache_fox_v, (0, 1, 3, 4, 2)).reshape(depth * n_pool, FOX_W, page)
    lft = jnp.swapaxes(cache_fox_logf, 2, 3).reshape(depth * n_pool * H_FOX, page)
    r3, s3 = [z.reshape(depth * n_pool, H_FOX, page) for z in _page_logf(lft)]

    yp = x_prompt.reshape(b * t, d)
    ys = x_sample.reshape(bd_ * td, d)
    eye = jnp.eye(H_FOX, dtype=F32)
    outs = [[] for _ in range(12)]
    for l in range(depth):
        w_big, w_small = _split_w_in(w_in[l])
        gain = ln_gain[l][None, :].astype(F32)
        wf, wg, wm, wo = [z[l].astype(BF16) for z in (w_fox_br, w_gdn_br, w_mem_br, w_out)]
        gdn_gain = gdn_out_gain[l][None, :].astype(F32)
        mq_gain = mem_q_gain[l][None, :].astype(F32)

        p, p16, ps = _in_proj(yp, gain, w_big, w_small, min(1024, b * t))
        kn, fv, lf, qa, ka, va = _fox_prep_prompt(p, ps, fox_q_gain[l], fox_k_gain[l], fox_f_bias[l], b, t)
        o_fox = _fox_attn_prompt(qa, ka, va, b, t).reshape(b * t, FOX_W)
        gq, gk, gv, gg = _gdn_prep(p, ps, jnp.zeros((b, 8, 3 * GDN_W), F32), gdn_conv_w[l], gdn_A_log[l],
                                   gdn_dt_bias[l], b, t)
        o_gdn, s_new = _gdn(gq, gk, gv, gg, jnp.zeros((b, H_GDN, DK_GDN, DV_GDN), F32), b, t)
        mk, mv = _mem_kv(mem_prompt.reshape(b * n_mem, d), mem_norm_gain[l][None, :].astype(F32),
                         w_mem_kv[l].astype(BF16), mem_k_gain[l][None, :].astype(F32))
        o_mem = _mem_attn(p, mk.reshape(b, n_mem, MEM_W), mv.reshape(b, n_mem, MEM_W), mq_gain, b, t)
        conv_tail = p.reshape(b, t, -1)[:, t - (CONV_W - 1):, GQ * FOX_W:GQ * FOX_W + 3 * GDN_W]
        yp = _out_proj(o_fox, o_gdn, o_mem, p, p16, yp, wf, wg, wm, wo, gdn_gain)
        for lst, val in zip(outs[:7], (kn.reshape(b, t, H_FOX, DH_FOX), fv.reshape(b, t, H_FOX, DH_FOX),
                                       lf[:, :, :H_FOX], s_new, conv_tail,
                                       mk.reshape(b, n_mem, H_MEM, DH_MEM), mv.reshape(b, n_mem, H_MEM, DH_MEM))):
            lst.append(val)

        p, p16, ps = _in_proj(ys, gain, w_big, w_small, bd_ * td)
        qn, kn, lf, cn = _fox_sample_prep(p, ps, fox_q_gain[l], fox_k_gain[l], fox_f_bias[l], td)
        fv = p[:, FV * FOX_W:(FV + 1) * FOX_W]
        q4 = qn.reshape(bd_, td, H_FOX, DH_FOX).transpose(0, 2, 1, 3)
        qbd = (q4[:, :, :, None, :] * eye[None, :, None, :, None]).reshape(bd_, H_FOX * td, FOX_W).astype(BF16)
        cn3 = cn.reshape(bd_, td, LANES)[:, :, :H_FOX].transpose(0, 2, 1)
        cnq = jnp.broadcast_to(cn3.reshape(bd_, H_FOX * td, 1), (bd_, H_FOX * td, LANES))
        cnk = jnp.broadcast_to(cn3[:, :, None, :], (bd_, H_FOX, td, td)).reshape(bd_, H_FOX * td, td)
        cnk = jnp.pad(cnk, ((0, 0), (0, 0), (0, LANES - td)))
        pad_rows = lambda z: jnp.pad(z.reshape(bd_, td, FOX_W), ((0, 0), (0, page - td), (0, 0))).astype(BF16).transpose(0, 2, 1)
        pidx = (l * n_pool + page_table).reshape(-1).astype(jnp.int32)
        o_fox = _fox_decode(pidx, qbd, cnq, cnk, pad_rows(kn), pad_rows(fv), kc, vc, r3, s3, n_pages, td)
        o_fox = o_fox.reshape(bd_ * td, FOX_W)

        init = jnp.pad(state_gdn_conv[l].astype(F32), ((0, 0), (8 - (CONV_W - 1), 0), (0, 0)))
        gq, gk, gv, gg = _gdn_prep(p, ps, init, gdn_conv_w[l], gdn_A_log[l], gdn_dt_bias[l], bd_, td)
        padc = lambda z: jnp.pad(z.reshape(bd_, td, -1), ((0, 0), (0, c - td), (0, 0))).reshape(bd_ * c, -1)
        o_gdn, s_new = _gdn(padc(gq), padc(gk), padc(gv), padc(gg), state_gdn[l].astype(F32), bd_, c)
        o_gdn = o_gdn.reshape(bd_, c, GDN_W)[:, :td].reshape(bd_ * td, GDN_W)
        o_mem = _mem_attn(p, cache_mem_k[l].reshape(bd_, n_mem, MEM_W), cache_mem_v[l].reshape(bd_, n_mem, MEM_W),
                          mq_gain, bd_, td)
        conv_tail = jnp.concatenate([state_gdn_conv[l].astype(F32),
                                     p.reshape(bd_, td, -1)[:, :, GQ * FOX_W:GQ * FOX_W + 3 * GDN_W]],
                                    axis=1)[:, td:]
        ys = _out_proj(o_fox, o_gdn, o_mem, p, p16, ys, wf, wg, wm, wo, gdn_gain)
        for lst, val in zip(outs[7:], (kn.reshape(bd_, td, H_FOX, DH_FOX), fv.reshape(bd_, td, H_FOX, DH_FOX),
                                       lf.reshape(bd_, td, LANES)[:, :, :H_FOX], s_new, conv_tail)):
            lst.append(val)

    return (yp.reshape(b, t, d), ys.reshape(bd_, td, d)) + tuple(jnp.stack(z) for z in outs)
```

```python
import functools

import numpy as np
import jax
import jax.numpy as jnp
from jax import lax
from jax.experimental import pallas as pl
from jax.experimental.pallas import tpu as pltpu

F32, BF16 = jnp.float32, jnp.bfloat16

H_FOX, DH_FOX = 8, 64
FOX_W = H_FOX * DH_FOX
H_GDN, DK_GDN, DV_GDN = 4, 128, 128
GDN_W = H_GDN * DV_GDN
CONV_W = 4
GDN_CHUNK = 64
H_MEM, DH_MEM = 4, 128
MEM_W = H_MEM * DH_MEM
N_BRANCH = 3
EPS = 1e-6
NEG = -1e30
LANES = 128
VMEM_LIMIT = 56 * 1024 * 1024
DECODE_PAGES_PER_STEP = 32

FQ, FK, FV, GQ, GK, GV, MQ, FZ = range(8)
GZ, MZ = 0, 1
GATE0 = 1
P_HALF = 8 * FOX_W
LOG2E = 1.4426950408889634
SM_FF, SM_GA, SM_GB = 0, 8, 12


def _params(*sem):
    return pltpu.CompilerParams(dimension_semantics=sem, vmem_limit_bytes=VMEM_LIMIT)


def _dot(a, b):
    return jnp.dot(a, b, preferred_element_type=F32)


def _dot_nt(a, b):
    return lax.dot_general(a, b, (((1,), (1,)), ((), ())), preferred_element_type=F32)


def _dot_tn(a, b):
    return lax.dot_general(a, b, (((0,), (0,)), ((), ())), preferred_element_type=F32)


def _split2(x):
    hi = x.astype(BF16)
    lo = (x - hi.astype(F32)).astype(BF16)
    return hi, lo


def _split3(x):
    hi = x.astype(BF16)
    r = x - hi.astype(F32)
    mid = r.astype(BF16)
    lo = (r - mid.astype(F32)).astype(BF16)
    return hi, mid, lo


def _dot3_lhs_exact(m, x):
    hi, mid, lo = _split3(x)
    return _dot(m, hi) + _dot(m, mid) + _dot(m, lo)


def _dot3_rhs_exact(x, m):
    hi, mid, lo = _split3(x)
    return _dot(hi, m) + _dot(mid, m) + _dot(lo, m)


def _sigmoid(x):
    return 1.0 / (1.0 + jnp.exp(-x))


def _silu(x):
    return x * _sigmoid(x)


def _softplus(x):
    return jnp.maximum(x, 0.0) + jnp.log1p(jnp.exp(-jnp.abs(x)))


def _log_sigmoid(x):
    return jnp.minimum(x, 0.0) - jnp.log1p(jnp.exp(-jnp.abs(x)))


def _lane_bcast(x, lane):
    return jnp.broadcast_to(x[:, lane:lane + 1], x.shape)


def _headnorm128(x, gain, nheads):
    outs = []
    for h in range(nheads):
        seg = x[:, h * LANES:(h + 1) * LANES]
        ms = jnp.mean(seg * seg, axis=-1, keepdims=True)
        outs.append(seg * lax.rsqrt(ms + EPS) * gain)
    return jnp.concatenate(outs, axis=1)


def _in_proj_kernel(x_ref, g_ref, w_ref, ws_ref, p32_ref, p16_ref, ps_ref, h_ref, *, n32):
    j = pl.program_id(1)

    @pl.when(j == 0)
    def _():
        x = x_ref[...]
        ms = jnp.mean(x * x, axis=-1, keepdims=True)
        h = (x * lax.rsqrt(ms + EPS) * g_ref[...]).astype(BF16)
        h_ref[...] = h
        ps_ref[...] = _dot(h, ws_ref[...])

    res = _dot(h_ref[...], w_ref[j])

    @pl.when(j < n32)
    def _():
        p32_ref[...] = res

    @pl.when(j >= n32)
    def _():
        p16_ref[...] = res.astype(BF16)


def _in_proj(x, gain, w_big, w_small, tm):
    n, d = x.shape
    ntile, _, tn = w_big.shape
    n32 = P_HALF // tn
    n16 = ntile - n32
    return pl.pallas_call(
        functools.partial(_in_proj_kernel, n32=n32),
        grid=(n // tm, ntile),
        in_specs=[pl.BlockSpec((tm, d), lambda i, j: (i, 0)),
                  pl.BlockSpec((1, d), lambda i, j: (0, 0)),
                  pl.BlockSpec((ntile, d, tn), lambda i, j: (0, 0, 0)),
                  pl.BlockSpec((d, LANES), lambda i, j: (0, 0))],
        out_specs=[pl.BlockSpec((tm, tn), lambda i, j: (i, jnp.minimum(j, n32 - 1))),
                   pl.BlockSpec((tm, tn), lambda i, j: (i, jnp.maximum(j - n32, 0))),
                   pl.BlockSpec((tm, LANES), lambda i, j: (i, 0))],
        out_shape=[jax.ShapeDtypeStruct((n, P_HALF), F32), jax.ShapeDtypeStruct((n, n16 * tn), BF16),
                   jax.ShapeDtypeStruct((n, LANES), F32)],
        scratch_shapes=[pltpu.VMEM((tm, d), BF16)],
        compiler_params=_params("parallel", "arbitrary"),
        name="in_proj")(x, gain, w_big, w_small)


def _fox_headnorm(x, gain, bd):
    hi, lo = _split2(x * x)
    ss = _dot(hi, bd) + _dot(lo, bd)
    return x * lax.rsqrt(ss * (1.0 / DH_FOX) + EPS) * gain


def _fox_prep_kernel(fq_ref, fk_ref, fv_ref, ps_ref, qg_ref, kg_ref, fb_ref, bd_ref, tri_ref,
                     selq_ref, selk_ref, selv_ref, qrow_ref, krow_ref,
                     kn_ref, v_ref, lf_ref, qa_ref, ka_ref, va_ref, carry_ref):
    @pl.when(pl.program_id(1) == 0)
    def _():
        carry_ref[...] = jnp.zeros_like(carry_ref)

    bd = bd_ref[...]
    qn = _fox_headnorm(fq_ref[...], qg_ref[...], bd) * (DH_FOX ** -0.5 * LOG2E)
    kn = _fox_headnorm(fk_ref[...], kg_ref[...], bd)
    fv = fv_ref[...]
    kn_ref[0] = kn
    v_ref[0] = fv
    lf = _log_sigmoid(ps_ref[...] + fb_ref[...])
    lf_ref[0] = lf
    c = _dot3_lhs_exact(tri_ref[...], lf) + carry_ref[0:1, :]
    tm = c.shape[0]
    carry_ref[...] = jnp.broadcast_to(c[tm - 1:tm, :], carry_ref.shape)
    chi, cmid, clo = _split3(c * LOG2E)
    qa =_dot(jnp.concatenate([qn.astype(BF16), chi, cmid, clo], axis=1), selq_ref[...]) + qrow_ref[...]
    ka = _dot(jnp.concatenate([kn.astype(BF16), chi, cmid, clo], axis=1), selk_ref[...]) + krow_ref[...]
    qa_ref[0] = qa.astype(BF16)
    ka_ref[0] = ka.astype(BF16)
    vt = _dot_nt(selv_ref[...], fv.astype(BF16))
    r = lax.broadcasted_iota(jnp.int32, vt.shape, 0) & (2 * LANES - 1)
    va_ref[0, 0] = jnp.where((r == DH_FOX) | (r == LANES), 1.0, vt).astype(BF16)


def _fox_consts(tm):
    a = np.arange(FOX_W)
    bd = (a[:, None] // DH_FOX == a[None, :] // DH_FOX).astype(np.float32)
    r = np.arange(tm)
    tri = (r[None, :] <= r[:, None]).astype(np.float32)
    aw = H_FOX * LANES
    selq = np.zeros((FOX_W + 3 * LANES, aw), np.float32)
    selk = np.zeros((FOX_W + 3 * LANES, aw), np.float32)
    selv = np.zeros((aw, FOX_W), np.float32)
    qrow = np.zeros((1, aw), np.float32)
    krow = np.zeros((1, aw), np.float32)
    for h in range(H_FOX):
        for d in range(DH_FOX):
            selq[h * DH_FOX + d, h * LANES + d] = 1.0
            selk[h * DH_FOX + d, h * LANES + d] = 1.0
            selv[h * LANES + (h % 2) * DH_FOX + d, h * DH_FOX + d] = 1.0
        for e in range(3):
            selq[FOX_W + e * LANES + h, h * LANES + DH_FOX + e] = 1.0
            selk[FOX_W + e * LANES + h, h * LANES + DH_FOX + 3 + e] = -1.0
            qrow[0, h * LANES + DH_FOX + 3 + e] = 1.0
            krow[0, h * LANES + DH_FOX + e] = 1.0
    bf = lambda z: jnp.asarray(z, BF16)
    return bf(bd), bf(tri), bf(selq), bf(selk), bf(selv), jnp.asarray(qrow), jnp.asarray(krow)


def _tile_gain(g, reps):
    return jnp.tile(g.astype(F32), reps)[None, :]


def _pad_row(vals, offset):
    row = jnp.zeros((LANES,), F32)
    return lax.dynamic_update_slice(row, vals.astype(F32), (offset,))[None, :]


def _fox_prep_prompt(p, ps, qgain, kgain, fbias, b, t):
    tm = min(512, t)
    nt = t // tm
    bd, tri, selq, selk, selv, qrow, krow = _fox_consts(tm)
    aw = H_FOX * LANES
    row = lambda blk: pl.BlockSpec((tm, FOX_W), lambda i, j, blk=blk: (i * nt + j, blk))
    const = lambda shape: pl.BlockSpec(shape, lambda i, j: (0,) * len(shape))
    tok = lambda w: pl.BlockSpec((1, tm, w), lambda i, j: (i, j, 0))
    return pl.pallas_call(
        _fox_prep_kernel,
        grid=(b, nt),
        in_specs=[row(FQ), row(FK), row(FV),
                  pl.BlockSpec((tm, LANES), lambda i, j: (i * nt + j, 0)),
                  const((1, FOX_W)), const((1, FOX_W)), const((1, LANES)),
                  const(bd.shape), const(tri.shape), const(selq.shape), const(selk.shape), const(selv.shape),
                  const(qrow.shape), const(krow.shape)],
        out_specs=[tok(FOX_W), tok(FOX_W), tok(LANES), tok(aw), tok(aw),
                   pl.BlockSpec((1, 1, aw, tm), lambda i, j: (i, j, 0, 0))],
        out_shape=[jax.ShapeDtypeStruct((b, t, FOX_W), F32), jax.ShapeDtypeStruct((b, t, FOX_W), F32),
                   jax.ShapeDtypeStruct((b, t, LANES), F32),
                   jax.ShapeDtypeStruct((b, t, aw), BF16), jax.ShapeDtypeStruct((b, t, aw), BF16),
                   jax.ShapeDtypeStruct((b, nt, aw, tm), BF16)],
        scratch_shapes=[pltpu.VMEM((8, LANES), F32)],
        compiler_params=_params("parallel", "arbitrary"),
        name="fox_prep")(p, p, p, ps, _tile_gain(qgain, H_FOX), _tile_gain(kgain, H_FOX), _pad_row(fbias, SM_FF),
                         bd, tri, selq, selk, selv, qrow, krow)


def _fox_attn_kernel(q_ref, k_ref, vt_ref, o_ref, m_ref, acc_ref, sa_ref, sb_ref, *, tk):
    qi = pl.program_id(2)
    tq = q_ref.shape[1]
    heads = [slice(hh * LANES, (hh + 1) * LANES) for hh in range(2)]
    m_ref[...] = jnp.full(m_ref.shape, NEG, F32)
    acc_ref[...] = jnp.zeros_like(acc_ref)

    def scores(ki, hh):
        r0 = pl.multiple_of(ki * tk, tk)
        return _dot_nt(k_ref[0, pl.ds(r0, tk), heads[hh]], q_ref[0, :, heads[hh]])

    def accumulate(st, ki, hh):
        m_prev = m_ref[hh]
        m_new = jnp.maximum(m_prev, jnp.max(st, axis=0, keepdims=True))
        pt = jnp.exp2(st - m_new).astype(BF16)
        acc_ref[hh] = jnp.exp2(m_prev - m_new) * acc_ref[hh] + _dot(vt_ref[0, ki, heads[hh], :], pt)
        m_ref[hh] = m_new

    def causal(st):
        krow = lax.broadcasted_iota(jnp.int32, (tk, tq), 0)
        qcol = lax.broadcasted_iota(jnp.int32, (tk, tq), 1)
        return jnp.where(krow <= qcol, st, NEG)

    for hh in range(2):
        sa_ref[hh] = scores(0, hh)

    def pair(j, carry):
        for hh in range(2):
            sb_ref[hh] = scores(2 * j + 1, hh)
            accumulate(sa_ref[hh], 2 * j, hh)
        for hh in range(2):
            sa_ref[hh] = scores(2 * j + 2, hh)
            accumulate(sb_ref[hh], 2 * j + 1, hh)
        return carry

    lax.fori_loop(0, qi // 2, pair, 0)

    @pl.when(qi % 2 == 1)
    def _():
        for hh in range(2):
            sb_ref[hh] = scores(qi, hh)
            accumulate(sa_ref[hh], qi - 1, hh)
        for hh in range(2):
            accumulate(causal(sb_ref[hh]), qi, hh)

    @pl.when(qi % 2 == 0)
    def _():
        for hh in range(2):
            accumulate(causal(sa_ref[hh]), qi, hh)

    a0, a1 = acc_ref[0], acc_ref[1]
    row = lax.broadcasted_iota(jnp.int32, a0.shape, 0)
    ot = jnp.where(row < DH_FOX, a0 / a0[DH_FOX:DH_FOX + 1, :], a1 / a1[0:1, :])
    o_ref[0] = ot.T


def _fox_attn_prompt(qa, ka, vat, b, t):
    tq = vat.shape[3]
    pair = 2 * LANES
    return pl.pallas_call(
        functools.partial(_fox_attn_kernel, tk=tq),
        grid=(b, H_FOX // 2, t // tq),
        in_specs=[pl.BlockSpec((1, tq, pair), lambda i, h, q: (i, q, h)),
                  pl.BlockSpec((1, t, pair), lambda i, h, q: (i, 0, h)),
                  pl.BlockSpec((1, t // tq, pair, tq), lambda i, h, q: (i, 0, h, 0))],
        out_specs=pl.BlockSpec((1, tq, LANES), lambda i, h, q: (i, q, h)),
        out_shape=jax.ShapeDtypeStruct((b, t, FOX_W), F32),
        scratch_shapes=[pltpu.VMEM((2, 1, tq), F32), pltpu.VMEM((2, LANES, tq), F32),
                        pltpu.VMEM((2, tq, tq), F32), pltpu.VMEM((2, tq, tq), F32)],
        compiler_params=_params("parallel", "parallel", "arbitrary"),
        name="fox_attn")(qa, ka, vat)


def _fox_sample_prep_kernel(fq_ref, fk_ref, ps_ref, qg_ref, kg_ref, fb_ref, bd_ref, tri_ref,
                            qn_ref, kn_ref, lf_ref, cn_ref):
    bd = bd_ref[...]
    qn_ref[...] = _fox_headnorm(fq_ref[...], qg_ref[...], bd) * (DH_FOX ** -0.5)
    kn_ref[...] = _fox_headnorm(fk_ref[...], kg_ref[...], bd)
    lf = _log_sigmoid(ps_ref[...] + fb_ref[...])
    lf_ref[...] = lf
    cn_ref[...] = _dot3_lhs_exact(tri_ref[...], lf)


def _fox_sample_prep(p, ps, qgain, kgain, fbias, td):
    n = p.shape[0]
    a = np.arange(FOX_W)
    bd = jnp.asarray((a[:, None] // DH_FOX == a[None, :] // DH_FOX), BF16)
    r = np.arange(n)
    tri = jnp.asarray((r[None, :] <= r[:, None]) & (r[None, :] // td == r[:, None] // td), BF16)
    full = lambda shape: pl.BlockSpec(shape, lambda i: (0,) * len(shape))
    return pl.pallas_call(
        _fox_sample_prep_kernel,
        grid=(1,),
        in_specs=[pl.BlockSpec((n, FOX_W), lambda i: (0, FQ)), pl.BlockSpec((n, FOX_W), lambda i: (0, FK)),
                  full((n, LANES)), full((1, FOX_W)), full((1, FOX_W)), full((1, LANES)),
                  full(bd.shape), full(tri.shape)],
        out_specs=[full((n, FOX_W)), full((n, FOX_W)), full((n, LANES)), full((n, LANES))],
        out_shape=[jax.ShapeDtypeStruct((n, FOX_W), F32), jax.ShapeDtypeStruct((n, FOX_W), F32),
                   jax.ShapeDtypeStruct((n, LANES), F32), jax.ShapeDtypeStruct((n, LANES), F32)],
        compiler_params=_params("arbitrary"),
        name="fox_sample_prep")(p, p, ps, _tile_gain(qgain, H_FOX), _tile_gain(kgain, H_FOX),
                                _pad_row(fbias, SM_FF), bd, tri)


def _fox_decode_kernel(pidx_ref, q_ref, cn_ref, cnk_ref, knew_ref, vnew_ref, uo_ref, *rest, g, td):
    k_refs, v_refs, f_refs = rest[0:g], rest[g:2 * g], rest[2 * g:3 * g]
    o_ref, m_ref, l_ref, acc_ref, suf_ref = rest[3 * g:]
    p = pl.program_id(1)
    q = q_ref[0]
    cn = cn_ref[0]
    nrow = q.shape[0]

    @pl.when(p == 0)
    def _():
        s = _dot(q, knew_ref[0])
        qpos = lax.broadcasted_iota(jnp.int32, s.shape, 0) % td
        kpos = lax.broadcasted_iota(jnp.int32, s.shape, 1)
        s = jnp.where(kpos <= qpos, s + cn - cnk_ref[0], NEG)
        m = jnp.max(s, axis=-1, keepdims=True)
        pr = jnp.exp(s - m)
        m_ref[...] = m
        l_ref[...] = jnp.sum(pr, axis=-1, keepdims=True)
        acc_ref[...] = _dot_nt(pr.astype(BF16), vnew_ref[0])
        suf_ref[...] = jnp.zeros_like(suf_ref)

    lf = jnp.concatenate([f_refs[j][0] for j in range(g)], axis=0)
    rs = _dot3_rhs_exact(lf, uo_ref[...])
    suf = suf_ref[...]
    scores = []
    for j in range(g):
        z = rs[j * H_FOX:(j + 1) * H_FOX, 0:LANES] + suf
        bias = jnp.concatenate([jnp.broadcast_to(z[h:h + 1, :], (td, LANES)) for h in range(H_FOX)], axis=0)
        scores.append(_dot(q, k_refs[j][0].astype(BF16)) + bias + cn)
        suf = suf + rs[j * H_FOX:(j + 1) * H_FOX, LANES:]
    suf_ref[...] = suf
    m_prev = m_ref[...]
    m_new = m_prev
    for s in scores:
        m_new = jnp.maximum(m_new, jnp.max(s, axis=-1, keepdims=True))
    alpha = jnp.exp(m_prev - m_new)
    l = alpha * l_ref[...]
    acc = alpha * acc_ref[...]
    for j in range(g):
        pr = jnp.exp(scores[j] - m_new)
        l = l + jnp.sum(pr, axis=-1, keepdims=True)
        acc = acc + _dot_nt(pr.astype(BF16), v_refs[j][0].astype(BF16))
    m_ref[...] = m_new
    l_ref[...] = l
    acc_ref[...] = acc

    @pl.when(p == pl.num_programs(1) - 1)
    def _():
        o = acc / l
        head = lax.broadcasted_iota(jnp.int32, (td, FOX_W), 1) // DH_FOX
        out = jnp.zeros((td, FOX_W), F32)
        for h in range(H_FOX):
            out = out + jnp.where(head == h, o[h * td:(h + 1) * td, :], 0.0)
        o_ref[0] = out


def _fox_decode(pidx, qbd, cn, cnk, knew, vnew, kc, vc, fc, n_pages, td):
    bd_, nrow, _ = qbd.shape
    page = kc.shape[2]
    g = min(DECODE_PAGES_PER_STEP, n_pages)
    steps = n_pages // g
    j = np.arange(page)
    uo = jnp.asarray(np.concatenate([j[:, None] > j[None, :], np.ones((page, page), bool)], axis=1), BF16)

    def page_map(j):
        return lambda b, p, idx: (idx[b * n_pages + (n_pages - 1 - (p * g + j))], 0, 0)

    per_b = lambda shape: pl.BlockSpec((1,) + shape, lambda b, p, idx: (b, 0, 0))
    in_specs = [per_b((nrow, FOX_W)), per_b((nrow, LANES)), per_b((nrow, LANES)),
                per_b((FOX_W, page)), per_b((FOX_W, page)),
                pl.BlockSpec(uo.shape, lambda b, p, idx: (0, 0))]
    in_specs += [pl.BlockSpec((1, FOX_W, page), page_map(j)) for j in range(g)]
    in_specs += [pl.BlockSpec((1, FOX_W, page), page_map(j)) for j in range(g)]
    in_specs += [pl.BlockSpec((1, H_FOX, page), page_map(j)) for j in range(g)]
    grid_spec = pltpu.PrefetchScalarGridSpec(
        num_scalar_prefetch=1, grid=(bd_, steps), in_specs=in_specs,
        out_specs=pl.BlockSpec((1, td, FOX_W), lambda b, p, idx: (b, 0, 0)),
        scratch_shapes=[pltpu.VMEM((nrow, 1), F32), pltpu.VMEM((nrow, 1), F32), pltpu.VMEM((nrow, FOX_W), F32),
                        pltpu.VMEM((H_FOX, LANES), F32)])
    return pl.pallas_call(
        functools.partial(_fox_decode_kernel, g=g, td=td),
        grid_spec=grid_spec,
        out_shape=jax.ShapeDtypeStruct((bd_, td, FOX_W), F32),
        compiler_params=_params("parallel", "arbitrary"),
        name="fox_decode")(pidx, qbd, cn, cnk, knew, vnew, uo, *([kc] * g), *([vc] * g), *([fc] * g))


def _gdn_prep_kernel(gq_ref, gk_ref, gv_ref, ps_ref, init_ref, w_ref, alog_ref, dtb_ref,
                     q_ref, k_ref, v_ref, g_ref, ext_ref):
    tm = gq_ref.shape[0]

    @pl.when(pl.program_id(1) == 0)
    def _():
        ext_ref[0:8, :] = init_ref[0]

    @pl.when(pl.program_id(1) > 0)
    def _():
        ext_ref[0:8, :] = ext_ref[tm:tm + 8, :]

    ext_ref[8:8 + tm, 0:GDN_W] = gq_ref[...]
    ext_ref[8:8 + tm, GDN_W:2 * GDN_W] = gk_ref[...]
    ext_ref[8:8 + tm, 2 * GDN_W:3 * GDN_W] = gv_ref[...]
    conv = ext_ref[8:8 + tm, :] * w_ref[CONV_W - 1:CONV_W, :]
    for i in range(CONV_W - 1):
        off = 8 - (CONV_W - 1) + i
        conv = conv + ext_ref[off:off + tm, :] * w_ref[i:i + 1, :]
    s = _silu(conv)

    def l2(x):
        outs = []
        for h in range(H_GDN):
            seg = x[:, h * LANES:(h + 1) * LANES]
            outs.append(seg * lax.rsqrt(jnp.sum(seg * seg, axis=-1, keepdims=True) + EPS))
        return jnp.concatenate(outs, axis=1)

    q_ref[...] = l2(s[:, 0:GDN_W]) * (DK_GDN ** -0.5)
    k_ref[...] = l2(s[:, GDN_W:2 * GDN_W])
    v_ref[...] = s[:, 2 * GDN_W:3 * GDN_W]
    ps = ps_ref[...]
    lane = lax.broadcasted_iota(jnp.int32, ps.shape, 1)
    gate = -jnp.exp(alog_ref[...]) * _softplus(ps + dtb_ref[...])
    beta = _sigmoid(ps)
    g_ref[...] = jnp.where((lane >= SM_GA) & (lane < SM_GB), gate,
                           jnp.where((lane >= SM_GB) & (lane < SM_GB + H_GDN), beta, 0.0))


def _gdn_prep(p, ps, init, conv_w, a_log, dt_bias, b, t):
    tm = min(512, t)
    nt = t // tm
    n = b * t
    row = lambda blk: pl.BlockSpec((tm, GDN_W), lambda i, j, blk=blk: (i * nt + j, blk))
    out = pl.BlockSpec((tm, GDN_W), lambda i, j: (i * nt + j, 0))
    small = pl.BlockSpec((tm, LANES), lambda i, j: (i * nt + j, 0))
    const = lambda shape: pl.BlockSpec(shape, lambda i, j: (0,) * len(shape))
    return pl.pallas_call(
        _gdn_prep_kernel,
        grid=(b, nt),
        in_specs=[row(GQ), row(GK), row(GV), small,
                  pl.BlockSpec((1, 8, 3 * GDN_W), lambda i, j: (i, 0, 0)),
                  const((CONV_W, 3 * GDN_W)), const((1, LANES)), const((1, LANES))],
        out_specs=[out, out, out, small],
        out_shape=[jax.ShapeDtypeStruct((n, GDN_W), F32)] * 3 + [jax.ShapeDtypeStruct((n, LANES), F32)],
        scratch_shapes=[pltpu.VMEM((tm + 8, 3 * GDN_W), F32)],
        compiler_params=_params("parallel", "arbitrary"),
        name="gdn_prep")(p, p, p, ps, init, conv_w.astype(F32), _pad_row(a_log, SM_GA), _pad_row(dt_bias, SM_GA))


def _gdn_chunk_kernel(q_ref, k_ref, v_ref, g_ref, tri_ref,
                      m_ref, qk_ref, u_ref, w_ref, qg_ref, kg_ref, egl_ref, *, nch):
    c = GDN_CHUNK
    tri = tri_ref[...]
    ri = lax.broadcasted_iota(jnp.int32, (c, c), 0)
    ci = lax.broadcasted_iota(jnp.int32, (c, c), 1)
    lane = lax.broadcasted_iota(jnp.int32, (c, LANES), 1)
    for cc in range(nch):
        rows = slice(cc * c, (cc + 1) * c)
        gt = g_ref[rows, :]
        gcum = _dot3_lhs_exact(tri, gt)
        for h in range(H_GDN):
            cols = slice(h * LANES, (h + 1) * LANES)
            gcol = _lane_bcast(gcum, SM_GA + h)
            beta = _lane_bcast(gt, SM_GB + h)
            hi, mid, lo = [z.astype(F32) for z in _split3(gcol)]
            a = jnp.where(lane == 0, hi, jnp.where(lane == 1, mid, jnp.where(lane == 2, lo,
                          jnp.where(lane < 6, 1.0, 0.0))))
            bm = jnp.where(lane < 3, 1.0, jnp.where(lane == 3, -hi, jnp.where(lane == 4, -mid,
                           jnp.where(lane == 5, -lo, 0.0))))
            diff = _dot_nt(a.astype(BF16), bm.astype(BF16))
            decay = jnp.exp(jnp.where(ci <= ri, diff, NEG))
            qh, kh, vh = q_ref[rows, cols], k_ref[rows, cols], v_ref[rows, cols]
            kb = kh * beta
            khb = kh.astype(BF16)
            m_ref[cc * H_GDN + h] = jnp.where(ci < ri, _dot_nt(kb.astype(BF16), khb) * decay, 0.0)
            qk_ref[cc * H_GDN + h] = _dot_nt(qh.astype(BF16), khb) * decay
            eg = jnp.exp(gcol)
            glast = gcol[c - 1:c, :]
            u_ref[rows, cols] = vh * beta
            w_ref[rows, cols] = kb * eg
            qg_ref[rows, cols] = (qh * eg).astype(BF16)
            kg_ref[rows, cols] = (kh * jnp.exp(glast - gcol)).astype(BF16)
            egl_ref[cc * H_GDN + h] = jnp.broadcast_to(jnp.exp(glast), (8, LANES))


def _gdn_chunk(q, k, v, g):
    n = q.shape[0]
    c = GDN_CHUNK
    tm = min(256, n)
    nch = tm // c
    r = np.arange(c)
    tri = jnp.asarray(r[None, :] <= r[:, None], BF16)
    nc = n // c * H_GDN
    row = pl.BlockSpec((tm, GDN_W), lambda i: (i, 0))
    blk = lambda shape: pl.BlockSpec((nch * H_GDN,) + shape, lambda i: (i, 0, 0))
    return pl.pallas_call(
        functools.partial(_gdn_chunk_kernel, nch=nch),
        grid=(n // tm,),
        in_specs=[row, row, row, pl.BlockSpec((tm, LANES), lambda i: (i, 0)), pl.BlockSpec((c, c), lambda i: (0, 0))],
        out_specs=[blk((c, c)), blk((c, c)), row, row, row, row, blk((8, LANES))],
        out_shape=[jax.ShapeDtypeStruct((nc, c, c), F32), jax.ShapeDtypeStruct((nc, c, c), F32)]
                  + [jax.ShapeDtypeStruct((n, GDN_W), F32)] * 2 + [jax.ShapeDtypeStruct((n, GDN_W), BF16)] * 2
                  + [jax.ShapeDtypeStruct((nc, 8, LANES), F32)],
        compiler_params=_params("parallel"),
        name="gdn_chunk")(q, k, v, g, tri)


def _gdn_inv_kernel(m_ref, x_ref):
    c = GDN_CHUNK
    for i in range(c):
        ext = 8 * (i // 8 + 1)
        e_i = (lax.broadcasted_iota(jnp.int32, (ext, LANES), 0) == i).astype(F32)

        def body(j, acc, i=i, ext=ext):
            return acc - m_ref[i, pl.ds(j, 1), :] * x_ref[j, 0:ext, :]

        x_ref[i, 0:ext, :] = lax.fori_loop(0, i, body, e_i, unroll=min(max(i, 1), 8))
        if ext < c:
            x_ref[i, ext:c, :] = jnp.zeros((c - ext, LANES), F32)


def _gdn_inv(mt):
    c = GDN_CHUNK
    nc = mt.shape[2]
    spec = pl.BlockSpec((c, c, LANES), lambda i: (0, 0, i))
    return pl.pallas_call(
        _gdn_inv_kernel,
        grid=(nc // LANES,),
        in_specs=[spec], out_specs=spec,
        out_shape=jax.ShapeDtypeStruct((c, c, nc), F32),
        compiler_params=_params("parallel"),
        name="gdn_inv")(mt)


def _gdn_scan_kernel(x_ref, qk_ref, u_ref, w_ref, qg_ref, kg_ref, egl_ref, s0_ref, o_ref, sout_ref,
                     s_ref, us_ref, wq_ref, *, nch):
    c = GDN_CHUNK

    @pl.when(pl.program_id(1) == 0)
    def _():
        s_ref[...] = s0_ref[0]

    heads = [slice(h * LANES, (h + 1) * LANES) for h in range(H_GDN)]

    def rows(cc):
        return cc * c if isinstance(cc, int) else pl.multiple_of(cc * c, c)

    def solve(cc, slot):
        r0 = rows(cc)
        for h in range(H_GDN):
            rhs = jnp.concatenate([u_ref[pl.ds(r0, c), heads[h]], w_ref[pl.ds(r0, c), heads[h]]], axis=1)
            xh, xl = _split2(x_ref[cc * H_GDN + h])
            rh, rl = _split2(rhs)
            sol = _dot(xh, rh) + _dot(xh, rl) + _dot(xl, rh)
            us_ref[slot, :, heads[h]] = sol[:, 0:DV_GDN]
            wq_ref[slot, 0:c, heads[h]] = sol[:, DV_GDN:].astype(BF16)
            wq_ref[slot, c:2 * c, heads[h]] = qg_ref[pl.ds(r0, c), heads[h]].astype(BF16)

    def advance(cc, slot):
        r0 = rows(cc)
        sts = [s_ref[h] for h in range(H_GDN)]
        boths = [_dot(wq_ref[slot, :, heads[h]], sts[h].astype(BF16)) for h in range(H_GDN)]
        dbs = [(us_ref[slot, :, heads[h]] - boths[h][0:c]).astype(BF16) for h in range(H_GDN)]
        for h in range(H_GDN):
            egl = jnp.broadcast_to(egl_ref[cc * H_GDN + h][0:1, :], (DK_GDN, DV_GDN))
            s_ref[h] = sts[h] * egl + _dot_tn(kg_ref[pl.ds(r0, c), heads[h]].astype(BF16), dbs[h])
        for h in range(H_GDN):
            o_ref[pl.ds(r0, c), heads[h]] = boths[h][c:] + _dot(qk_ref[cc * H_GDN + h].astype(BF16), dbs[h])

    solve(0, 0)

    def pair(j, carry):
        solve(2 * j + 1, 1)
        advance(2 * j, 0)
        solve(jnp.minimum(2 * j + 2, nch - 1), 0)
        advance(2 * j + 1, 1)
        return carry

    lax.fori_loop(0, nch // 2, pair, 0)
    if nch % 2:
        advance(nch - 1, 0)

    @pl.when(pl.program_id(1) == pl.num_programs(1) - 1)
    def _():
        sout_ref[0] = s_ref[...]


def _gdn_scan(x, qk, u, w, qg, kg, egl, s0, b, t):
    c = GDN_CHUNK
    tg = min(1024, t)
    ng = t // tg
    nch = tg // c
    row = pl.BlockSpec((tg, GDN_W), lambda i, j: (i * ng + j, 0))
    blk = lambda shape: pl.BlockSpec((nch * H_GDN,) + shape, lambda i, j: (i * ng + j, 0, 0))
    st = pl.BlockSpec((1, H_GDN, DK_GDN, DV_GDN), lambda i, j: (i, 0, 0, 0))
    return pl.pallas_call(
        functools.partial(_gdn_scan_kernel, nch=nch),
        grid=(b, ng),
        in_specs=[blk((c, c)), blk((c, c)), row, row, row, row, blk((8, LANES)), st],
        out_specs=[row, st],
        out_shape=[jax.ShapeDtypeStruct((b * t, GDN_W), F32), jax.ShapeDtypeStruct((b, H_GDN, DK_GDN, DV_GDN), F32)],
        scratch_shapes=[pltpu.VMEM((H_GDN, DK_GDN, DV_GDN), F32), pltpu.VMEM((2, c, GDN_W), F32),
                        pltpu.VMEM((2, 2 * c, GDN_W), BF16)],
        compiler_params=_params("parallel", "arbitrary"),
        name="gdn_scan")(x, qk, u, w, qg, kg, egl, s0)


def _gdn(q, k, v, g, s0, b, t):
    c = GDN_CHUNK
    m, qk, u, w, qg, kg, egl = _gdn_chunk(q, k, v, g)
    nc = m.shape[0]
    pad = (-nc) % LANES
    mt = jnp.pad(m.reshape(nc, c * c), ((0, pad), (0, 0))).T.reshape(c, c, nc + pad)
    x = _gdn_inv(mt).reshape(c * c, nc + pad).T[:nc].reshape(nc, c, c)
    return _gdn_scan(x, qk, u, w, qg, kg, egl, s0, b, t)


def _mem_kv_kernel(m_ref, g_ref, w_ref, kg_ref, mk_ref, mv_ref):
    x = m_ref[...]
    ms = jnp.mean(x * x, axis=-1, keepdims=True)
    h = (x * lax.rsqrt(ms + EPS) * g_ref[...]).astype(BF16)
    kv = _dot(h, w_ref[...])
    mk_ref[...] = _headnorm128(kv[:, 0:MEM_W], kg_ref[...], H_MEM)
    mv_ref[...] = kv[:, MEM_W:]


def _mem_kv(mem, gain, w_kv, k_gain):
    n, d = mem.shape
    tm = 256
    return pl.pallas_call(
        _mem_kv_kernel,
        grid=(n // tm,),
        in_specs=[pl.BlockSpec((tm, d), lambda i: (i, 0)), pl.BlockSpec((1, d), lambda i: (0, 0)),
                  pl.BlockSpec((d, 2 * MEM_W), lambda i: (0, 0)), pl.BlockSpec((1, LANES), lambda i: (0, 0))],
        out_specs=[pl.BlockSpec((tm, MEM_W), lambda i: (i, 0))] * 2,
        out_shape=[jax.ShapeDtypeStruct((n, MEM_W), F32)] * 2,
        compiler_params=_params("parallel"),
        name="mem_kv")(mem, gain, w_kv, k_gain)


def _mem_attn_kernel(q_ref, mk_ref, mv_ref, g_ref, o_ref, *, cast, nb):
    dt = BF16 if cast else F32
    tm = q_ref.shape[0] // nb
    qn_all = _headnorm128(q_ref[...], g_ref[...], H_MEM)
    for bb in range(nb):
        qn = qn_all[bb * tm:(bb + 1) * tm]
        outs = []
        for h in range(H_MEM):
            cols = slice(h * LANES, (h + 1) * LANES)
            s = _dot_nt(qn[:, cols].astype(dt), mk_ref[bb, :, cols].astype(dt)) * (DH_MEM ** -0.5)
            p = jnp.exp(s - jnp.max(s, axis=-1, keepdims=True))
            p = p / jnp.sum(p, axis=-1, keepdims=True)
            outs.append(_dot(p.astype(dt), mv_ref[bb, :, cols].astype(dt)))
        o_ref[bb * tm:(bb + 1) * tm, :] = jnp.concatenate(outs, axis=1)


def _mem_attn(p, mk, mv, q_gain, b, t):
    tm = min(512, t)
    nt = t // tm
    nb = 4 if (nt == 1 and tm < 64 and b % 4 == 0) else 1
    n_mem = mk.shape[1]
    kv = pl.BlockSpec((nb, n_mem, MEM_W), lambda i, j: (i, 0, 0))
    return pl.pallas_call(
        functools.partial(_mem_attn_kernel, cast=tm >= 16, nb=nb),
        grid=(b // nb, nt),
        in_specs=[pl.BlockSpec((nb * tm, MEM_W), lambda i, j: (i * nt + j, MQ)), kv, kv,
                  pl.BlockSpec((1, LANES), lambda i, j: (0, 0))],
        out_specs=pl.BlockSpec((nb * tm, MEM_W), lambda i, j: (i * nt + j, 0)),
        out_shape=jax.ShapeDtypeStruct((b * t, MEM_W), F32),
        compiler_params=_params("parallel", "parallel"),
        name="mem_attn")(p, mk, mv, q_gain)


def _out_kernel(of_ref, og_ref, om_ref, fz_ref, gz_ref, mz_ref, g0_ref, g1_ref, g2_ref, x_ref,
                wf_ref, wg_ref, wm_ref, wo_ref, gg_ref, y_ref):
    f32 = lambda ref: ref[...].astype(F32)
    a = (of_ref[...] * _silu(fz_ref[...])).astype(BF16)
    b = (_headnorm128(og_ref[...], gg_ref[...], H_GDN) * _silu(f32(gz_ref))).astype(BF16)
    c = (om_ref[...] * _silu(f32(mz_ref))).astype(BF16)
    merged = (_sigmoid(f32(g0_ref)) * _dot(a, wf_ref[...]) + _sigmoid(f32(g1_ref)) * _dot(b, wg_ref[...])
              + _sigmoid(f32(g2_ref)) * _dot(c, wm_ref[...]))
    y_ref[...] = x_ref[...] + _dot(merged.astype(BF16), wo_ref[...])


def _out_proj(o_fox, o_gdn, o_mem, p32, p16, x, wf, wg, wm, wo, gdn_gain):
    n, d = x.shape
    tm = min(256, n)
    branch = pl.BlockSpec((tm, FOX_W), lambda i: (i, 0))
    pz = lambda blk: pl.BlockSpec((tm, FOX_W), lambda i, blk=blk: (i, blk))
    pg = lambda blk: pl.BlockSpec((tm, d), lambda i, blk=blk: (i, blk))
    const = lambda shape: pl.BlockSpec(shape, lambda i: (0, 0))
    return pl.pallas_call(
        _out_kernel,
        grid=(n // tm,),
        in_specs=[branch, branch, branch, pz(FZ), pz(GZ), pz(MZ), pg(GATE0), pg(GATE0 + 1), pg(GATE0 + 2),
                  pl.BlockSpec((tm, d), lambda i: (i, 0)),
                  const(wf.shape), const(wg.shape), const(wm.shape), const(wo.shape), const((1, LANES))],
        out_specs=pl.BlockSpec((tm, d), lambda i: (i, 0)),
        out_shape=jax.ShapeDtypeStruct((n, d), F32),
        compiler_params=_params("parallel"),
        name="out_proj")(o_fox, o_gdn, o_mem, p32, p16, p16, p16, p16, p16, x, wf, wg, wm, wo, gdn_gain)


def _split_w_in(w):
    o = np.cumsum([0, FOX_W, FOX_W, FOX_W, FOX_W, H_FOX, 3 * GDN_W, GDN_W, H_GDN, H_GDN, MEM_W, MEM_W])
    big = jnp.concatenate([w[:, o[0]:o[3]], w[:, o[5]:o[6]], w[:, o[9]:o[10]], w[:, o[3]:o[4]],
                           w[:, o[6]:o[7]], w[:, o[10]:o[11]], w[:, o[11]:]], axis=1).astype(BF16)
    assert big.shape[1] == 2 * P_HALF
    tn = 1024
    big = big.reshape(big.shape[0], big.shape[1] // tn, tn).transpose(1, 0, 2)
    small = jnp.concatenate([w[:, o[4]:o[5]], w[:, o[7]:o[9]]], axis=1)
    small = jnp.pad(small, ((0, 0), (0, LANES - small.shape[1]))).astype(BF16)
    return big, small


def kernel(x_prompt, x_sample, cache_fox_k, cache_fox_v, cache_fox_logf, state_gdn, state_gdn_conv,
           cache_mem_k, cache_mem_v, page_table, mem_prompt, ln_gain, w_in, fox_q_gain, fox_k_gain,
           fox_f_bias, gdn_conv_w, gdn_A_log, gdn_dt_bias, gdn_out_gain, mem_norm_gain, w_mem_kv,
           mem_q_gain, mem_k_gain, w_fox_br, w_gdn_br, w_mem_br, w_out):
    b, t, d = x_prompt.shape
    bd_, td, _ = x_sample.shape
    depth, n_pool, page = cache_fox_k.shape[:3]
    n_pages = page_table.shape[1]
    n_mem = mem_prompt.shape[1]
    c = GDN_CHUNK
    assert page == LANES and t % c == 0 and td <= c and td % 8 == 0

    kc = jnp.transpose(cache_fox_k, (0, 1, 3, 4, 2)).reshape(depth * n_pool, FOX_W, page)
    vc = jnp.transpose(cache_fox_v, (0, 1, 3, 4, 2)).reshape(depth * n_pool, FOX_W, page)
    fc = jnp.swapaxes(cache_fox_logf, 2, 3).reshape(depth * n_pool, H_FOX, page)

    yp = x_prompt.reshape(b * t, d)
    ys = x_sample.reshape(bd_ * td, d)
    eye = jnp.eye(H_FOX, dtype=F32)
    outs = [[] for _ in range(12)]
    for l in range(depth):
        w_big, w_small = _split_w_in(w_in[l])
        gain = ln_gain[l][None, :].astype(F32)
        wf, wg, wm, wo = [z[l].astype(BF16) for z in (w_fox_br, w_gdn_br, w_mem_br, w_out)]
        gdn_gain = gdn_out_gain[l][None, :].astype(F32)
        mq_gain = mem_q_gain[l][None, :].astype(F32)

        p, p16, ps = _in_proj(yp, gain, w_big, w_small, min(512, b * t))
        kn, fv, lf, qa, ka, va = _fox_prep_prompt(p, ps, fox_q_gain[l], fox_k_gain[l], fox_f_bias[l], b, t)
        o_fox = _fox_attn_prompt(qa, ka, va, b, t).reshape(b * t, FOX_W)
        gq, gk, gv, gg = _gdn_prep(p, ps, jnp.zeros((b, 8, 3 * GDN_W), F32), gdn_conv_w[l], gdn_A_log[l],
                                   gdn_dt_bias[l], b, t)
        o_gdn, s_new = _gdn(gq, gk, gv, gg, jnp.zeros((b, H_GDN, DK_GDN, DV_GDN), F32), b, t)
        mk, mv = _mem_kv(mem_prompt.reshape(b * n_mem, d), mem_norm_gain[l][None, :].astype(F32),
                         w_mem_kv[l].astype(BF16), mem_k_gain[l][None, :].astype(F32))
        o_mem = _mem_attn(p, mk.reshape(b, n_mem, MEM_W), mv.reshape(b, n_mem, MEM_W), mq_gain, b, t)
        conv_tail = p.reshape(b, t, -1)[:, t - (CONV_W - 1):, GQ * FOX_W:GQ * FOX_W + 3 * GDN_W]
        yp = _out_proj(o_fox, o_gdn, o_mem, p, p16, yp, wf, wg, wm, wo, gdn_gain)
        for lst, val in zip(outs[:7], (kn.reshape(b, t, H_FOX, DH_FOX), fv.reshape(b, t, H_FOX, DH_FOX),
                                       lf[:, :, :H_FOX], s_new, conv_tail,
                                       mk.reshape(b, n_mem, H_MEM, DH_MEM), mv.reshape(b, n_mem, H_MEM, DH_MEM))):
            lst.append(val)

        p, p16, ps = _in_proj(ys, gain, w_big, w_small, bd_ * td)
        qn, kn, lf, cn = _fox_sample_prep(p, ps, fox_q_gain[l], fox_k_gain[l], fox_f_bias[l], td)
        fv = p[:, FV * FOX_W:(FV + 1) * FOX_W]
        q4 = qn.reshape(bd_, td, H_FOX, DH_FOX).transpose(0, 2, 1, 3)
        qbd = (q4[:, :, :, None, :] * eye[None, :, None, :, None]).reshape(bd_, H_FOX * td, FOX_W).astype(BF16)
        cn3 = cn.reshape(bd_, td, LANES)[:, :, :H_FOX].transpose(0, 2, 1)
        cnq = jnp.broadcast_to(cn3.reshape(bd_, H_FOX * td, 1), (bd_, H_FOX * td, LANES))
        cnk = jnp.broadcast_to(cn3[:, :, None, :], (bd_, H_FOX, td, td)).reshape(bd_, H_FOX * td, td)
        cnk = jnp.pad(cnk, ((0, 0), (0, 0), (0, LANES - td)))
        pad_rows = lambda z: jnp.pad(z.reshape(bd_, td, FOX_W), ((0, 0), (0, page - td), (0, 0))).astype(BF16).transpose(0, 2, 1)
        pidx = (l * n_pool + page_table).reshape(-1).astype(jnp.int32)
        o_fox = _fox_decode(pidx, qbd, cnq, cnk, pad_rows(kn), pad_rows(fv), kc, vc, fc, n_pages, td)
        o_fox = o_fox.reshape(bd_ * td, FOX_W)

        init = jnp.pad(state_gdn_conv[l].astype(F32), ((0, 0), (8 - (CONV_W - 1), 0), (0, 0)))
        gq, gk, gv, gg = _gdn_prep(p, ps, init, gdn_conv_w[l], gdn_A_log[l], gdn_dt_bias[l], bd_, td)
        padc = lambda z: jnp.pad(z.reshape(bd_, td, -1), ((0, 0), (0, c - td), (0, 0))).reshape(bd_ * c, -1)
        o_gdn, s_new = _gdn(padc(gq), padc(gk), padc(gv), padc(gg), state_gdn[l].astype(F32), bd_, c)
        o_gdn = o_gdn.reshape(bd_, c, GDN_W)[:, :td].reshape(bd_ * td, GDN_W)
        o_mem = _mem_attn(p, cache_mem_k[l].reshape(bd_, n_mem, MEM_W), cache_mem_v[l].reshape(bd_, n_mem, MEM_W),
                          mq_gain, bd_, td)
        conv_tail = jnp.concatenate([state_gdn_conv[l].astype(F32),
                                     p.reshape(bd_, td, -1)[:, :, GQ * FOX_W:GQ * FOX_W + 3 * GDN_W]],
                                    axis=1)[:, td:]
        ys = _out_proj(o_fox, o_gdn, o_mem, p, p16, ys, wf, wg, wm, wo, gdn_gain)
        for lst, val in zip(outs[7:], (kn.reshape(bd_, td, H_FOX, DH_FOX), fv.reshape(bd_, td, H_FOX, DH_FOX),
                                       lf.reshape(bd_, td, LANES)[:, :, :H_FOX], s_new, conv_tail)):
            lst.append(val)

    return (yp.reshape(b, t, d), ys.reshape(bd_, td, d)) + tuple(jnp.stack(z) for z in outs)
```

```python
import functools

import numpy as np
import jax
import jax.numpy as jnp
from jax import lax
from jax.experimental import pallas as pl
from jax.experimental.pallas import tpu as pltpu

F32, BF16 = jnp.float32, jnp.bfloat16

H_FOX, DH_FOX = 8, 64
FOX_W = H_FOX * DH_FOX
H_GDN, DK_GDN, DV_GDN = 4, 128, 128
GDN_W = H_GDN * DV_GDN
CONV_W = 4
GDN_CHUNK = 64
H_MEM, DH_MEM = 4, 128
MEM_W = H_MEM * DH_MEM
N_BRANCH = 3
EPS = 1e-6
NEG = -1e30
LANES = 128
VMEM_LIMIT = 56 * 1024 * 1024
DECODE_PAGES_PER_STEP = 32

FQ, FK, FV, GQ, GK, GV, MQ, FZ = range(8)
GZ, MZ = 0, 1
GATE0 = 1
P_HALF = 8 * FOX_W
LOG2E = 1.4426950408889634
SM_FF, SM_GA, SM_GB = 0, 8, 12


def _params(*sem):
    return pltpu.CompilerParams(dimension_semantics=sem, vmem_limit_bytes=VMEM_LIMIT)


def _dot(a, b):
    return jnp.dot(a, b, preferred_element_type=F32)


def _dot_nt(a, b):
    return lax.dot_general(a, b, (((1,), (1,)), ((), ())), preferred_element_type=F32)


def _dot_tn(a, b):
    return lax.dot_general(a, b, (((0,), (0,)), ((), ())), preferred_element_type=F32)


def _split2(x):
    hi = x.astype(BF16)
    lo = (x - hi.astype(F32)).astype(BF16)
    return hi, lo


def _split3(x):
    hi = x.astype(BF16)
    r = x - hi.astype(F32)
    mid = r.astype(BF16)
    lo = (r - mid.astype(F32)).astype(BF16)
    return hi, mid, lo


def _dot3_lhs_exact(m, x):
    hi, mid, lo = _split3(x)
    return _dot(m, hi) + _dot(m, mid) + _dot(m, lo)


def _dot3_rhs_exact(x, m):
    hi, mid, lo = _split3(x)
    return _dot(hi, m) + _dot(mid, m) + _dot(lo, m)


def _sigmoid(x):
    return 1.0 / (1.0 + jnp.exp(-x))


def _silu(x):
    return x * _sigmoid(x)


def _softplus(x):
    return jnp.maximum(x, 0.0) + jnp.log1p(jnp.exp(-jnp.abs(x)))


def _log_sigmoid(x):
    return jnp.minimum(x, 0.0) - jnp.log1p(jnp.exp(-jnp.abs(x)))


def _lane_bcast(x, lane):
    return jnp.broadcast_to(x[:, lane:lane + 1], x.shape)


def _headnorm128(x, gain, nheads):
    outs = []
    for h in range(nheads):
        seg = x[:, h * LANES:(h + 1) * LANES]
        ms = jnp.mean(seg * seg, axis=-1, keepdims=True)
        outs.append(seg * lax.rsqrt(ms + EPS) * gain)
    return jnp.concatenate(outs, axis=1)


def _in_proj_kernel(x_ref, g_ref, w_ref, ws_ref, p32_ref, p16_ref, ps_ref, h_ref, *, n32):
    j = pl.program_id(1)

    @pl.when(j == 0)
    def _():
        x = x_ref[...]
        ms = jnp.mean(x * x, axis=-1, keepdims=True)
        h = (x * lax.rsqrt(ms + EPS) * g_ref[...]).astype(BF16)
        h_ref[...] = h
        ps_ref[...] = _dot(h, ws_ref[...])

    res = _dot(h_ref[...], w_ref[j])

    @pl.when(j < n32)
    def _():
        p32_ref[...] = res

    @pl.when(j >= n32)
    def _():
        p16_ref[...] = res.astype(BF16)


def _in_proj(x, gain, w_big, w_small, tm):
    n, d = x.shape
    ntile, _, tn = w_big.shape
    n32 = P_HALF // tn
    n16 = ntile - n32
    return pl.pallas_call(
        functools.partial(_in_proj_kernel, n32=n32),
        grid=(n // tm, ntile),
        in_specs=[pl.BlockSpec((tm, d), lambda i, j: (i, 0)),
                  pl.BlockSpec((1, d), lambda i, j: (0, 0)),
                  pl.BlockSpec((ntile, d, tn), lambda i, j: (0, 0, 0), pipeline_mode=pl.Buffered(1)),
                  pl.BlockSpec((d, LANES), lambda i, j: (0, 0))],
        out_specs=[pl.BlockSpec((tm, tn), lambda i, j: (i, jnp.minimum(j, n32 - 1))),
                   pl.BlockSpec((tm, tn), lambda i, j: (i, jnp.maximum(j - n32, 0))),
                   pl.BlockSpec((tm, LANES), lambda i, j: (i, 0))],
        out_shape=[jax.ShapeDtypeStruct((n, P_HALF), F32), jax.ShapeDtypeStruct((n, n16 * tn), BF16),
                   jax.ShapeDtypeStruct((n, LANES), F32)],
        scratch_shapes=[pltpu.VMEM((tm, d), BF16)],
        compiler_params=_params("parallel", "arbitrary"),
        name="in_proj")(x, gain, w_big, w_small)


N_EXT = 3


def _fox_headnorm(x, gain, bd):
    ss = _dot((x * x).astype(BF16), bd)
    return x * lax.rsqrt(ss * (1.0 / DH_FOX) + EPS) * gain


def _fox_prep_kernel(fq_ref, fk_ref, fv_ref, ps_ref, qg_ref, kg_ref, fb_ref, bd_ref, tri_ref, sele_ref, selv_ref,
                     knt_ref, vt32_ref, lf_ref, qa_ref, ka_ref, va_ref, carry_ref):
    @pl.when(pl.program_id(1) == 0)
    def _():
        carry_ref[...] = jnp.zeros_like(carry_ref)

    bd = bd_ref[...]
    qn = _fox_headnorm(fq_ref[...], qg_ref[...], bd) * (DH_FOX ** -0.5 * LOG2E)
    kn = _fox_headnorm(fk_ref[...], kg_ref[...], bd)
    fv = fv_ref[...]
    knt_ref[0] = kn.T
    vt32_ref[0] = fv.T
    lf = _log_sigmoid(ps_ref[...] + fb_ref[...])
    lf_ref[0] = lf
    c = _dot3_lhs_exact(tri_ref[...], lf) + carry_ref[0:1, :]
    tm = c.shape[0]
    carry_ref[...] = jnp.broadcast_to(c[tm - 1:tm, :], carry_ref.shape)
    ext = _dot(jnp.concatenate(_split3(c * LOG2E), axis=1), sele_ref[...])
    lane = lax.broadcasted_iota(jnp.int32, (tm, LANES), 1)
    in_k_ext = (lane >= DH_FOX + N_EXT) & (lane < DH_FOX + 2 * N_EXT)
    q_ones = in_k_ext.astype(F32)
    k_ones = ((lane >= DH_FOX) & (lane < DH_FOX + N_EXT)).astype(F32)
    for h in range(H_FOX):
        pair = slice((h // 2) * LANES, (h // 2 + 1) * LANES)
        bq, bk = qn[:, pair], kn[:, pair]
        if h % 2:
            bq, bk = pltpu.roll(bq, DH_FOX, 1), pltpu.roll(bk, DH_FOX, 1)
        eq = pltpu.roll(ext[:, 0:LANES], DH_FOX - 8 * h, 1)
        ek = pltpu.roll(ext[:, LANES:2 * LANES], DH_FOX - 8 * h, 1)
        cols = slice(h * LANES, (h + 1) * LANES)
        qa_ref[0, :, cols] = jnp.where(lane < DH_FOX, bq, jnp.where(lane < DH_FOX + N_EXT, eq, q_ones)).astype(BF16)
        ka_ref[0, :, cols] = jnp.where(lane < DH_FOX, bk, jnp.where(in_k_ext, ek, k_ones)).astype(BF16)
    vt = _dot_nt(selv_ref[...], fv.astype(BF16))
    r = lax.broadcasted_iota(jnp.int32, vt.shape, 0) & (2 * LANES - 1)
    va_ref[0, 0] = jnp.where((r == DH_FOX) | (r == LANES), 1.0, vt).astype(BF16)


def _fox_consts(tm):
    a = np.arange(FOX_W)
    bd = (a[:, None] // DH_FOX == a[None, :] // DH_FOX).astype(np.float32)
    r = np.arange(tm)
    tri = (r[None, :] <= r[:, None]).astype(np.float32)
    aw = H_FOX * LANES
    sele = np.zeros((N_EXT * LANES, 2 * LANES), np.float32)
    selv = np.zeros((aw, FOX_W), np.float32)
    for h in range(H_FOX):
        for d in range(DH_FOX):
            selv[h * LANES + (h % 2) * DH_FOX + d, h * DH_FOX + d] = 1.0
        for e in range(N_EXT):
            sele[e * LANES + h, 8 * h + e] = 1.0
            sele[e * LANES + h, LANES + 8 * h + N_EXT + e] = -1.0
    bf = lambda z: jnp.asarray(z, BF16)
    return bf(bd), bf(tri), bf(sele), bf(selv)


def _tile_gain(g, reps):
    return jnp.tile(g.astype(F32), reps)[None, :]


def _pad_row(vals, offset):
    row = jnp.zeros((LANES,), F32)
    return lax.dynamic_update_slice(row, vals.astype(F32), (offset,))[None, :]


def _fox_prep_prompt(p, ps, qgain, kgain, fbias, b, t):
    tm = min(512, t)
    nt = t // tm
    bd, tri, sele, selv = _fox_consts(tm)
    aw = H_FOX * LANES
    row = lambda blk: pl.BlockSpec((tm, FOX_W), lambda i, j, blk=blk: (i * nt + j, blk))
    const = lambda shape: pl.BlockSpec(shape, lambda i, j: (0,) * len(shape))
    tok = lambda w: pl.BlockSpec((1, tm, w), lambda i, j: (i, j, 0))
    tr = pl.BlockSpec((1, FOX_W, tm), lambda i, j: (i, 0, j))
    return pl.pallas_call(
        _fox_prep_kernel,
        grid=(b, nt),
        in_specs=[row(FQ), row(FK), row(FV),
                  pl.BlockSpec((tm, LANES), lambda i, j: (i * nt + j, 0)),
                  const((1, FOX_W)), const((1, FOX_W)), const((1, LANES)),
                  const(bd.shape), const(tri.shape), const(sele.shape), const(selv.shape)],
        out_specs=[tr, tr, tok(LANES), tok(aw), tok(aw),
                   pl.BlockSpec((1, 1, aw, tm), lambda i, j: (i, j, 0, 0))],
        out_shape=[jax.ShapeDtypeStruct((b, FOX_W, t), F32), jax.ShapeDtypeStruct((b, FOX_W, t), F32),
                   jax.ShapeDtypeStruct((b, t, LANES), F32),
                   jax.ShapeDtypeStruct((b, t, aw), BF16), jax.ShapeDtypeStruct((b, t, aw), BF16),
                   jax.ShapeDtypeStruct((b, nt, aw, tm), BF16)],
        scratch_shapes=[pltpu.VMEM((8, LANES), F32)],
        compiler_params=_params("parallel", "arbitrary"),
        name="fox_prep")(p, p, p, ps, _tile_gain(qgain, H_FOX), _tile_gain(kgain, H_FOX), _pad_row(fbias, SM_FF),
                         bd, tri, sele, selv)


def _fox_attn_kernel(q_ref, k_ref, vt_ref, o_ref, m_ref, acc_ref, sa_ref, sb_ref, *, tk):
    qi = pl.program_id(2)
    tq = q_ref.shape[1]
    heads = [slice(hh * LANES, (hh + 1) * LANES) for hh in range(2)]
    m_ref[...] = jnp.full(m_ref.shape, NEG, F32)
    acc_ref[...] = jnp.zeros_like(acc_ref)

    def scores(ki, hh):
        r0 = pl.multiple_of(ki * tk, tk)
        return _dot_nt(k_ref[0, pl.ds(r0, tk), heads[hh]], q_ref[0, :, heads[hh]])

    def accumulate(st, ki, hh):
        m_prev = m_ref[hh]
        m_new = jnp.maximum(m_prev, jnp.max(st, axis=0, keepdims=True))
        pt = jnp.exp2(st - m_new).astype(BF16)
        acc_ref[hh] = jnp.exp2(m_prev - m_new) * acc_ref[hh] + _dot(vt_ref[0, ki, heads[hh], :], pt)
        m_ref[hh] = m_new

    def causal(st):
        krow = lax.broadcasted_iota(jnp.int32, (tk, tq), 0)
        qcol = lax.broadcasted_iota(jnp.int32, (tk, tq), 1)
        return jnp.where(krow <= qcol, st, NEG)

    for hh in range(2):
        sa_ref[hh] = scores(0, hh)

    def pair(j, carry):
        for hh in range(2):
            sb_ref[hh] = scores(2 * j + 1, hh)
            accumulate(sa_ref[hh], 2 * j, hh)
        for hh in range(2):
            sa_ref[hh] = scores(2 * j + 2, hh)
            accumulate(sb_ref[hh], 2 * j + 1, hh)
        return carry

    lax.fori_loop(0, qi // 2, pair, 0)

    @pl.when(qi % 2 == 1)
    def _():
        for hh in range(2):
            sb_ref[hh] = scores(qi, hh)
            accumulate(sa_ref[hh], qi - 1, hh)
        for hh in range(2):
            accumulate(causal(sb_ref[hh]), qi, hh)

    @pl.when(qi % 2 == 0)
    def _():
        for hh in range(2):
            accumulate(causal(sa_ref[hh]), qi, hh)

    a0, a1 = acc_ref[0], acc_ref[1]
    row = lax.broadcasted_iota(jnp.int32, a0.shape, 0)
    ot = jnp.where(row < DH_FOX, a0 / a0[DH_FOX:DH_FOX + 1, :], a1 / a1[0:1, :])
    o_ref[0] = ot.T


def _fox_attn_prompt(qa, ka, vat, b, t):
    tq = vat.shape[3]
    pair = 2 * LANES
    return pl.pallas_call(
        functools.partial(_fox_attn_kernel, tk=tq),
        grid=(b, H_FOX // 2, t // tq),
        in_specs=[pl.BlockSpec((1, tq, pair), lambda i, h, q: (i, q, h)),
                  pl.BlockSpec((1, t, pair), lambda i, h, q: (i, 0, h)),
                  pl.BlockSpec((1, t // tq, pair, tq), lambda i, h, q: (i, 0, h, 0))],
        out_specs=pl.BlockSpec((1, tq, LANES), lambda i, h, q: (i, q, h)),
        out_shape=jax.ShapeDtypeStruct((b, t, FOX_W), F32),
        scratch_shapes=[pltpu.VMEM((2, 1, tq), F32), pltpu.VMEM((2, LANES, tq), F32),
                        pltpu.VMEM((2, tq, tq), F32), pltpu.VMEM((2, tq, tq), F32)],
        compiler_params=_params("parallel", "parallel", "arbitrary"),
        name="fox_attn")(qa, ka, vat)


def _fox_sample_prep_kernel(fq_ref, fk_ref, ps_ref, qg_ref, kg_ref, fb_ref, bd_ref, tri_ref,
                            qn_ref, kn_ref, lf_ref, cn_ref):
    bd = bd_ref[...]
    qn_ref[...] = _fox_headnorm(fq_ref[...], qg_ref[...], bd) * (DH_FOX ** -0.5)
    kn_ref[...] = _fox_headnorm(fk_ref[...], kg_ref[...], bd)
    lf = _log_sigmoid(ps_ref[...] + fb_ref[...])
    lf_ref[...] = lf
    cn_ref[...] = _dot3_lhs_exact(tri_ref[...], lf)


def _fox_sample_prep(p, ps, qgain, kgain, fbias, td):
    n = p.shape[0]
    a = np.arange(FOX_W)
    bd = jnp.asarray((a[:, None] // DH_FOX == a[None, :] // DH_FOX), BF16)
    r = np.arange(n)
    tri = jnp.asarray((r[None, :] <= r[:, None]) & (r[None, :] // td == r[:, None] // td), BF16)
    full = lambda shape: pl.BlockSpec(shape, lambda i: (0,) * len(shape))
    return pl.pallas_call(
        _fox_sample_prep_kernel,
        grid=(1,),
        in_specs=[pl.BlockSpec((n, FOX_W), lambda i: (0, FQ)), pl.BlockSpec((n, FOX_W), lambda i: (0, FK)),
                  full((n, LANES)), full((1, FOX_W)), full((1, FOX_W)), full((1, LANES)),
                  full(bd.shape), full(tri.shape)],
        out_specs=[full((n, FOX_W)), full((n, FOX_W)), full((n, LANES)), full((n, LANES))],
        out_shape=[jax.ShapeDtypeStruct((n, FOX_W), F32), jax.ShapeDtypeStruct((n, FOX_W), F32),
                   jax.ShapeDtypeStruct((n, LANES), F32), jax.ShapeDtypeStruct((n, LANES), F32)],
        compiler_params=_params("arbitrary"),
        name="fox_sample_prep")(p, p, ps, _tile_gain(qgain, H_FOX), _tile_gain(kgain, H_FOX),
                                _pad_row(fbias, SM_FF), bd, tri)


def _fox_decode_kernel(pidx_ref, q_ref, cn_ref, cnk_ref, knew_ref, vnew_ref, uo_ref, *rest, g, td):
    k_refs, v_refs, f_refs = rest[0:g], rest[g:2 * g], rest[2 * g:3 * g]
    o_ref, m_ref, l_ref, acc_ref, suf_ref = rest[3 * g:]
    p = pl.program_id(1)
    q = q_ref[0]
    cn = cn_ref[0]
    nrow = q.shape[0]

    @pl.when(p == 0)
    def _():
        s = _dot(q, knew_ref[0])
        qpos = lax.broadcasted_iota(jnp.int32, s.shape, 0) % td
        kpos = lax.broadcasted_iota(jnp.int32, s.shape, 1)
        s = jnp.where(kpos <= qpos, s + cn - cnk_ref[0], NEG)
        m = jnp.max(s, axis=-1, keepdims=True)
        pr = jnp.exp(s - m)
        m_ref[...] = m
        l_ref[...] = jnp.sum(pr, axis=-1, keepdims=True)
        acc_ref[...] = _dot_nt(pr.astype(BF16), vnew_ref[0])
        suf_ref[...] = jnp.zeros_like(suf_ref)

    lf = jnp.concatenate([f_refs[j][0] for j in range(g)], axis=0)
    rs = _dot3_rhs_exact(lf, uo_ref[...])
    suf = suf_ref[...]
    scores = []
    for j in range(g):
        z = rs[j * H_FOX:(j + 1) * H_FOX, 0:LANES] + suf
        bias = jnp.concatenate([jnp.broadcast_to(z[h:h + 1, :], (td, LANES)) for h in range(H_FOX)], axis=0)
        scores.append(_dot(q, k_refs[j][0].astype(BF16)) + bias + cn)
        suf = suf + rs[j * H_FOX:(j + 1) * H_FOX, LANES:]
    suf_ref[...] = suf
    m_prev = m_ref[...]
    m_new = m_prev
    for s in scores:
        m_new = jnp.maximum(m_new, jnp.max(s, axis=-1, keepdims=True))
    alpha = jnp.exp(m_prev - m_new)
    l = alpha * l_ref[...]
    acc = alpha * acc_ref[...]
    for j in range(g):
        pr = jnp.exp(scores[j] - m_new)
        l = l + jnp.sum(pr, axis=-1, keepdims=True)
        acc = acc + _dot_nt(pr.astype(BF16), v_refs[j][0].astype(BF16))
    m_ref[...] = m_new
    l_ref[...] = l
    acc_ref[...] = acc

    @pl.when(p == pl.num_programs(1) - 1)
    def _():
        o = acc / l
        head = lax.broadcasted_iota(jnp.int32, (td, FOX_W), 1) // DH_FOX
        out = jnp.zeros((td, FOX_W), F32)
        for h in range(H_FOX):
            out = out + jnp.where(head == h, o[h * td:(h + 1) * td, :], 0.0)
        o_ref[0] = out


def _fox_decode(pidx, qbd, cn, cnk, knew, vnew, kc, vc, fc, n_pages, td):
    bd_, nrow, _ = qbd.shape
    page = kc.shape[2]
    g = min(DECODE_PAGES_PER_STEP, n_pages)
    steps = n_pages // g
    j = np.arange(page)
    uo = jnp.asarray(np.concatenate([j[:, None] > j[None, :], np.ones((page, page), bool)], axis=1), BF16)

    def page_map(j):
        return lambda b, p, idx: (idx[b * n_pages + (n_pages - 1 - (p * g + j))], 0, 0)

    per_b = lambda shape: pl.BlockSpec((1,) + shape, lambda b, p, idx: (b, 0, 0))
    in_specs = [per_b((nrow, FOX_W)), per_b((nrow, LANES)), per_b((nrow, LANES)),
                per_b((FOX_W, page)), per_b((FOX_W, page)),
                pl.BlockSpec(uo.shape, lambda b, p, idx: (0, 0))]
    in_specs += [pl.BlockSpec((1, FOX_W, page), page_map(j)) for j in range(g)]
    in_specs += [pl.BlockSpec((1, FOX_W, page), page_map(j)) for j in range(g)]
    in_specs += [pl.BlockSpec((1, H_FOX, page), page_map(j)) for j in range(g)]
    grid_spec = pltpu.PrefetchScalarGridSpec(
        num_scalar_prefetch=1, grid=(bd_, steps), in_specs=in_specs,
        out_specs=pl.BlockSpec((1, td, FOX_W), lambda b, p, idx: (b, 0, 0)),
        scratch_shapes=[pltpu.VMEM((nrow, 1), F32), pltpu.VMEM((nrow, 1), F32), pltpu.VMEM((nrow, FOX_W), F32),
                        pltpu.VMEM((H_FOX, LANES), F32)])
    return pl.pallas_call(
        functools.partial(_fox_decode_kernel, g=g, td=td),
        grid_spec=grid_spec,
        out_shape=jax.ShapeDtypeStruct((bd_, td, FOX_W), F32),
        compiler_params=_params("parallel", "arbitrary"),
        name="fox_decode")(pidx, qbd, cn, cnk, knew, vnew, uo, *([kc] * g), *([vc] * g), *([fc] * g))


def _gdn_prep_kernel(gq_ref, gk_ref, gv_ref, ps_ref, init_ref, w_ref, alog_ref, dtb_ref,
                     q_ref, k_ref, v_ref, g_ref, ext_ref):
    tm = gq_ref.shape[0]

    @pl.when(pl.program_id(1) == 0)
    def _():
        ext_ref[0:8, :] = init_ref[0]

    @pl.when(pl.program_id(1) > 0)
    def _():
        ext_ref[0:8, :] = ext_ref[tm:tm + 8, :]

    ext_ref[8:8 + tm, 0:GDN_W] = gq_ref[...]
    ext_ref[8:8 + tm, GDN_W:2 * GDN_W] = gk_ref[...]
    ext_ref[8:8 + tm, 2 * GDN_W:3 * GDN_W] = gv_ref[...]
    conv = ext_ref[8:8 + tm, :] * w_ref[CONV_W - 1:CONV_W, :]
    for i in range(CONV_W - 1):
        off = 8 - (CONV_W - 1) + i
        conv = conv + ext_ref[off:off + tm, :] * w_ref[i:i + 1, :]
    s = _silu(conv)

    def l2(x):
        outs = []
        for h in range(H_GDN):
            seg = x[:, h * LANES:(h + 1) * LANES]
            outs.append(seg * lax.rsqrt(jnp.sum(seg * seg, axis=-1, keepdims=True) + EPS))
        return jnp.concatenate(outs, axis=1)

    q_ref[...] = l2(s[:, 0:GDN_W]) * (DK_GDN ** -0.5)
    k_ref[...] = l2(s[:, GDN_W:2 * GDN_W])
    v_ref[...] = s[:, 2 * GDN_W:3 * GDN_W]
    ps = ps_ref[...]
    lane = lax.broadcasted_iota(jnp.int32, ps.shape, 1)
    gate = -jnp.exp(alog_ref[...]) * _softplus(ps + dtb_ref[...])
    beta = _sigmoid(ps)
    g_ref[...] = jnp.where((lane >= SM_GA) & (lane < SM_GB), gate,
                           jnp.where((lane >= SM_GB) & (lane < SM_GB + H_GDN), beta, 0.0))


def _gdn_prep(p, ps, init, conv_w, a_log, dt_bias, b, t):
    tm = min(512, t)
    nt = t // tm
    n = b * t
    row = lambda blk: pl.BlockSpec((tm, GDN_W), lambda i, j, blk=blk: (i * nt + j, blk))
    out = pl.BlockSpec((tm, GDN_W), lambda i, j: (i * nt + j, 0))
    small = pl.BlockSpec((tm, LANES), lambda i, j: (i * nt + j, 0))
    const = lambda shape: pl.BlockSpec(shape, lambda i, j: (0,) * len(shape))
    return pl.pallas_call(
        _gdn_prep_kernel,
        grid=(b, nt),
        in_specs=[row(GQ), row(GK), row(GV), small,
                  pl.BlockSpec((1, 8, 3 * GDN_W), lambda i, j: (i, 0, 0)),
                  const((CONV_W, 3 * GDN_W)), const((1, LANES)), const((1, LANES))],
        out_specs=[out, out, out, small],
        out_shape=[jax.ShapeDtypeStruct((n, GDN_W), F32)] * 3 + [jax.ShapeDtypeStruct((n, LANES), F32)],
        scratch_shapes=[pltpu.VMEM((tm + 8, 3 * GDN_W), F32)],
        compiler_params=_params("parallel", "arbitrary"),
        name="gdn_prep")(p, p, p, ps, init, conv_w.astype(F32), _pad_row(a_log, SM_GA), _pad_row(dt_bias, SM_GA))


def _gdn_chunk_kernel(q_ref, k_ref, v_ref, g_ref, tri_ref,
                      m_ref, qk_ref, u_ref, w_ref, qg_ref, kg_ref, egl_ref, *, nch):
    c = GDN_CHUNK
    tri = tri_ref[...]
    ri = lax.broadcasted_iota(jnp.int32, (c, c), 0)
    ci = lax.broadcasted_iota(jnp.int32, (c, c), 1)
    lane = lax.broadcasted_iota(jnp.int32, (c, LANES), 1)
    for cc in range(nch):
        rows = slice(cc * c, (cc + 1) * c)
        gt = g_ref[rows, :]
        gcum = _dot3_lhs_exact(tri, gt)
        for h in range(H_GDN):
            cols = slice(h * LANES, (h + 1) * LANES)
            gcol = _lane_bcast(gcum, SM_GA + h)
            beta = _lane_bcast(gt, SM_GB + h)
            hi, mid, lo = [z.astype(F32) for z in _split3(gcol)]
            a = jnp.where(lane == 0, hi, jnp.where(lane == 1, mid, jnp.where(lane == 2, lo,
                          jnp.where(lane < 6, 1.0, 0.0))))
            bm = jnp.where(lane < 3, 1.0, jnp.where(lane == 3, -hi, jnp.where(lane == 4, -mid,
                           jnp.where(lane == 5, -lo, 0.0))))
            diff = _dot_nt(a.astype(BF16), bm.astype(BF16))
            decay = jnp.exp(jnp.where(ci <= ri, diff, NEG))
            qh, kh, vh = q_ref[rows, cols], k_ref[rows, cols], v_ref[rows, cols]
            kb = kh * beta
            khb = kh.astype(BF16)
            m_ref[cc * H_GDN + h] = jnp.where(ci < ri, _dot_nt(kb.astype(BF16), khb) * decay, 0.0)
            qk_ref[cc * H_GDN + h] = _dot_nt(qh.astype(BF16), khb) * decay
            eg = jnp.exp(gcol)
            glast = gcol[c - 1:c, :]
            u_ref[rows, cols] = vh * beta
            w_ref[rows, cols] = kb * eg
            qg_ref[rows, cols] = (qh * eg).astype(BF16)
            kg_ref[rows, cols] = (kh * jnp.exp(glast - gcol)).astype(BF16)
            egl_ref[cc * H_GDN + h] = jnp.broadcast_to(jnp.exp(glast), (8, LANES))


def _gdn_chunk(q, k, v, g):
    n = q.shape[0]
    c = GDN_CHUNK
    tm = min(256, n)
    nch = tm // c
    r = np.arange(c)
    tri = jnp.asarray(r[None, :] <= r[:, None], BF16)
    nc = n // c * H_GDN
    row = pl.BlockSpec((tm, GDN_W), lambda i: (i, 0))
    blk = lambda shape: pl.BlockSpec((nch * H_GDN,) + shape, lambda i: (i, 0, 0))
    return pl.pallas_call(
        functools.partial(_gdn_chunk_kernel, nch=nch),
        grid=(n // tm,),
        in_specs=[row, row, row, pl.BlockSpec((tm, LANES), lambda i: (i, 0)), pl.BlockSpec((c, c), lambda i: (0, 0))],
        out_specs=[blk((c, c)), blk((c, c)), row, row, row, row, blk((8, LANES))],
        out_shape=[jax.ShapeDtypeStruct((nc, c, c), F32), jax.ShapeDtypeStruct((nc, c, c), F32)]
                  + [jax.ShapeDtypeStruct((n, GDN_W), F32)] * 2 + [jax.ShapeDtypeStruct((n, GDN_W), BF16)] * 2
                  + [jax.ShapeDtypeStruct((nc, 8, LANES), F32)],
        compiler_params=_params("parallel"),
        name="gdn_chunk")(q, k, v, g, tri)


def _gdn_inv_kernel(m_ref, x_ref):
    c = GDN_CHUNK
    for i in range(c):
        ext = 8 * (i // 8 + 1)
        e_i = (lax.broadcasted_iota(jnp.int32, (ext, LANES), 0) == i).astype(F32)

        def body(j, acc, i=i, ext=ext):
            return acc - m_ref[i, pl.ds(j, 1), :] * x_ref[j, 0:ext, :]

        x_ref[i, 0:ext, :] = lax.fori_loop(0, i, body, e_i, unroll=min(max(i, 1), 8))
        if ext < c:
            x_ref[i, ext:c, :] = jnp.zeros((c - ext, LANES), F32)


def _gdn_inv(mt):
    c = GDN_CHUNK
    nc = mt.shape[2]
    spec = pl.BlockSpec((c, c, LANES), lambda i: (0, 0, i))
    return pl.pallas_call(
        _gdn_inv_kernel,
        grid=(nc // LANES,),
        in_specs=[spec], out_specs=spec,
        out_shape=jax.ShapeDtypeStruct((c, c, nc), F32),
        compiler_params=_params("parallel"),
        name="gdn_inv")(mt)


def _gdn_scan_kernel(x_ref, qk_ref, u_ref, w_ref, qg_ref, kg_ref, egl_ref, s0_ref, o_ref, sout_ref,
                     s_ref, us_ref, wq_ref, *, nch):
    c = GDN_CHUNK

    @pl.when(pl.program_id(1) == 0)
    def _():
        s_ref[...] = s0_ref[0]

    heads = [slice(h * LANES, (h + 1) * LANES) for h in range(H_GDN)]

    def rows(cc):
        return cc * c if isinstance(cc, int) else pl.multiple_of(cc * c, c)

    def solve(cc, slot):
        r0 = rows(cc)
        for h in range(H_GDN):
            rhs = jnp.concatenate([u_ref[pl.ds(r0, c), heads[h]], w_ref[pl.ds(r0, c), heads[h]]], axis=1)
            xh, xl = _split2(x_ref[cc * H_GDN + h])
            rh, rl = _split2(rhs)
            sol = _dot(xh, rh) + _dot(xh, rl) + _dot(xl, rh)
            us_ref[slot, :, heads[h]] = sol[:, 0:DV_GDN]
            wq_ref[slot, 0:c, heads[h]] = sol[:, DV_GDN:].astype(BF16)
            wq_ref[slot, c:2 * c, heads[h]] = qg_ref[pl.ds(r0, c), heads[h]].astype(BF16)

    def advance(cc, slot):
        r0 = rows(cc)
        sts = [s_ref[h] for h in range(H_GDN)]
        boths = [_dot(wq_ref[slot, :, heads[h]], sts[h].astype(BF16)) for h in range(H_GDN)]
        dbs = [(us_ref[slot, :, heads[h]] - boths[h][0:c]).astype(BF16) for h in range(H_GDN)]
        for h in range(H_GDN):
            egl = jnp.broadcast_to(egl_ref[cc * H_GDN + h][0:1, :], (DK_GDN, DV_GDN))
            s_ref[h] = sts[h] * egl + _dot_tn(kg_ref[pl.ds(r0, c), heads[h]].astype(BF16), dbs[h])
        for h in range(H_GDN):
            o_ref[pl.ds(r0, c), heads[h]] = boths[h][c:] + _dot(qk_ref[cc * H_GDN + h].astype(BF16), dbs[h])

    solve(0, 0)

    def pair(j, carry):
        solve(2 * j + 1, 1)
        advance(2 * j, 0)
        solve(jnp.minimum(2 * j + 2, nch - 1), 0)
        advance(2 * j + 1, 1)
        return carry

    lax.fori_loop(0, nch // 2, pair, 0)
    if nch % 2:
        advance(nch - 1, 0)

    @pl.when(pl.program_id(1) == pl.num_programs(1) - 1)
    def _():
        sout_ref[0] = s_ref[...]


def _gdn_scan(x, qk, u, w, qg, kg, egl, s0, b, t):
    c = GDN_CHUNK
    tg = min(1024, t)
    ng = t // tg
    nch = tg // c
    row = pl.BlockSpec((tg, GDN_W), lambda i, j: (i * ng + j, 0))
    blk = lambda shape: pl.BlockSpec((nch * H_GDN,) + shape, lambda i, j: (i * ng + j, 0, 0))
    st = pl.BlockSpec((1, H_GDN, DK_GDN, DV_GDN), lambda i, j: (i, 0, 0, 0))
    return pl.pallas_call(
        functools.partial(_gdn_scan_kernel, nch=nch),
        grid=(b, ng),
        in_specs=[blk((c, c)), blk((c, c)), row, row, row, row, blk((8, LANES)), st],
        out_specs=[row, st],
        out_shape=[jax.ShapeDtypeStruct((b * t, GDN_W), F32), jax.ShapeDtypeStruct((b, H_GDN, DK_GDN, DV_GDN), F32)],
        scratch_shapes=[pltpu.VMEM((H_GDN, DK_GDN, DV_GDN), F32), pltpu.VMEM((2, c, GDN_W), F32),
                        pltpu.VMEM((2, 2 * c, GDN_W), BF16)],
        compiler_params=_params("parallel", "arbitrary"),
        name="gdn_scan")(x, qk, u, w, qg, kg, egl, s0)


def _gdn(q, k, v, g, s0, b, t):
    c = GDN_CHUNK
    m, qk, u, w, qg, kg, egl = _gdn_chunk(q, k, v, g)
    nc = m.shape[0]
    pad = (-nc) % LANES
    mt = jnp.pad(m.reshape(nc, c * c), ((0, pad), (0, 0))).T.reshape(c, c, nc + pad)
    x = _gdn_inv(mt).reshape(c * c, nc + pad).T[:nc].reshape(nc, c, c)
    return _gdn_scan(x, qk, u, w, qg, kg, egl, s0, b, t)


def _mem_kv_kernel(m_ref, g_ref, w_ref, kg_ref, mk_ref, mv_ref):
    x = m_ref[...]
    ms = jnp.mean(x * x, axis=-1, keepdims=True)
    h = (x * lax.rsqrt(ms + EPS) * g_ref[...]).astype(BF16)
    kv = _dot(h, w_ref[...])
    mk_ref[...] = _headnorm128(kv[:, 0:MEM_W], kg_ref[...], H_MEM)
    mv_ref[...] = kv[:, MEM_W:]


def _mem_kv(mem, gain, w_kv, k_gain):
    n, d = mem.shape
    tm = 256
    return pl.pallas_call(
        _mem_kv_kernel,
        grid=(n // tm,),
        in_specs=[pl.BlockSpec((tm, d), lambda i: (i, 0)), pl.BlockSpec((1, d), lambda i: (0, 0)),
                  pl.BlockSpec((d, 2 * MEM_W), lambda i: (0, 0)), pl.BlockSpec((1, LANES), lambda i: (0, 0))],
        out_specs=[pl.BlockSpec((tm, MEM_W), lambda i: (i, 0))] * 2,
        out_shape=[jax.ShapeDtypeStruct((n, MEM_W), F32)] * 2,
        compiler_params=_params("parallel"),
        name="mem_kv")(mem, gain, w_kv, k_gain)


def _mem_attn_kernel(q_ref, mk_ref, mv_ref, g_ref, o_ref, *, cast, nb):
    dt = BF16 if cast else F32
    tm = q_ref.shape[0] // nb
    qn_all = _headnorm128(q_ref[...], g_ref[...], H_MEM)
    for bb in range(nb):
        qn = qn_all[bb * tm:(bb + 1) * tm]
        outs = []
        for h in range(H_MEM):
            cols = slice(h * LANES, (h + 1) * LANES)
            s = _dot_nt(qn[:, cols].astype(dt), mk_ref[bb, :, cols].astype(dt)) * (DH_MEM ** -0.5)
            p = jnp.exp(s - jnp.max(s, axis=-1, keepdims=True))
            p = p / jnp.sum(p, axis=-1, keepdims=True)
            outs.append(_dot(p.astype(dt), mv_ref[bb, :, cols].astype(dt)))
        o_ref[bb * tm:(bb + 1) * tm, :] = jnp.concatenate(outs, axis=1)


def _mem_attn(p, mk, mv, q_gain, b, t):
    tm = min(512, t)
    nt = t // tm
    nb = 4 if (nt == 1 and tm < 64 and b % 4 == 0) else 1
    n_mem = mk.shape[1]
    kv = pl.BlockSpec((nb, n_mem, MEM_W), lambda i, j: (i, 0, 0))
    return pl.pallas_call(
        functools.partial(_mem_attn_kernel, cast=tm >= 16, nb=nb),
        grid=(b // nb, nt),
        in_specs=[pl.BlockSpec((nb * tm, MEM_W), lambda i, j: (i * nt + j, MQ)), kv, kv,
                  pl.BlockSpec((1, LANES), lambda i, j: (0, 0))],
        out_specs=pl.BlockSpec((nb * tm, MEM_W), lambda i, j: (i * nt + j, 0)),
        out_shape=jax.ShapeDtypeStruct((b * t, MEM_W), F32),
        compiler_params=_params("parallel", "parallel"),
        name="mem_attn")(p, mk, mv, q_gain)


def _out_kernel(of_ref, og_ref, om_ref, fz_ref, gz_ref, mz_ref, g0_ref, g1_ref, g2_ref, x_ref,
                wf_ref, wg_ref, wm_ref, wo_ref, gg_ref, y_ref):
    f32 = lambda ref: ref[...].astype(F32)
    a = (of_ref[...] * _silu(fz_ref[...])).astype(BF16)
    b = (_headnorm128(og_ref[...], gg_ref[...], H_GDN) * _silu(f32(gz_ref))).astype(BF16)
    c = (om_ref[...] * _silu(f32(mz_ref))).astype(BF16)
    merged = (_sigmoid(f32(g0_ref)) * _dot(a, wf_ref[...]) + _sigmoid(f32(g1_ref)) * _dot(b, wg_ref[...])
              + _sigmoid(f32(g2_ref)) * _dot(c, wm_ref[...]))
    y_ref[...] = x_ref[...] + _dot(merged.astype(BF16), wo_ref[...])


def _out_proj(o_fox, o_gdn, o_mem, p32, p16, x, wf, wg, wm, wo, gdn_gain):
    n, d = x.shape
    tm = min(512, n)
    branch = pl.BlockSpec((tm, FOX_W), lambda i: (i, 0))
    pz = lambda blk: pl.BlockSpec((tm, FOX_W), lambda i, blk=blk: (i, blk))
    pg = lambda blk: pl.BlockSpec((tm, d), lambda i, blk=blk: (i, blk))
    const = lambda shape: pl.BlockSpec(shape, lambda i: (0, 0))
    return pl.pallas_call(
        _out_kernel,
        grid=(n // tm,),
        in_specs=[branch, branch, branch, pz(FZ), pz(GZ), pz(MZ), pg(GATE0), pg(GATE0 + 1), pg(GATE0 + 2),
                  pl.BlockSpec((tm, d), lambda i: (i, 0)),
                  const(wf.shape), const(wg.shape), const(wm.shape), const(wo.shape), const((1, LANES))],
        out_specs=pl.BlockSpec((tm, d), lambda i: (i, 0)),
        out_shape=jax.ShapeDtypeStruct((n, d), F32),
        compiler_params=_params("parallel"),
        name="out_proj")(o_fox, o_gdn, o_mem, p32, p16, p16, p16, p16, p16, x, wf, wg, wm, wo, gdn_gain)


def _split_w_in(w):
    o = np.cumsum([0, FOX_W, FOX_W, FOX_W, FOX_W, H_FOX, 3 * GDN_W, GDN_W, H_GDN, H_GDN, MEM_W, MEM_W])
    big = jnp.concatenate([w[:, o[0]:o[3]], w[:, o[5]:o[6]], w[:, o[9]:o[10]], w[:, o[3]:o[4]],
                           w[:, o[6]:o[7]], w[:, o[10]:o[11]], w[:, o[11]:]], axis=1).astype(BF16)
    assert big.shape[1] == 2 * P_HALF
    tn = 1024
    big = big.reshape(big.shape[0], big.shape[1] // tn, tn).transpose(1, 0, 2)
    small = jnp.concatenate([w[:, o[4]:o[5]], w[:, o[7]:o[9]]], axis=1)
    small = jnp.pad(small, ((0, 0), (0, LANES - small.shape[1]))).astype(BF16)
    return big, small


def kernel(x_prompt, x_sample, cache_fox_k, cache_fox_v, cache_fox_logf, state_gdn, state_gdn_conv,
           cache_mem_k, cache_mem_v, page_table, mem_prompt, ln_gain, w_in, fox_q_gain, fox_k_gain,
           fox_f_bias, gdn_conv_w, gdn_A_log, gdn_dt_bias, gdn_out_gain, mem_norm_gain, w_mem_kv,
           mem_q_gain, mem_k_gain, w_fox_br, w_gdn_br, w_mem_br, w_out):
    b, t, d = x_prompt.shape
    bd_, td, _ = x_sample.shape
    depth, n_pool, page = cache_fox_k.shape[:3]
    n_pages = page_table.shape[1]
    n_mem = mem_prompt.shape[1]
    c = GDN_CHUNK
    assert page == LANES and t % c == 0 and td <= c and td % 8 == 0

    kc = jnp.transpose(cache_fox_k, (0, 1, 3, 4, 2)).reshape(depth * n_pool, FOX_W, page)
    vc = jnp.transpose(cache_fox_v, (0, 1, 3, 4, 2)).reshape(depth * n_pool, FOX_W, page)
    fc = jnp.swapaxes(cache_fox_logf, 2, 3).reshape(depth * n_pool, H_FOX, page)

    yp = x_prompt.reshape(b * t, d)
    ys = x_sample.reshape(bd_ * td, d)
    eye = jnp.eye(H_FOX, dtype=F32)
    outs = [[] for _ in range(12)]
    for l in range(depth):
        w_big, w_small = _split_w_in(w_in[l])
        gain = ln_gain[l][None, :].astype(F32)
        wf, wg, wm, wo = [z[l].astype(BF16) for z in (w_fox_br, w_gdn_br, w_mem_br, w_out)]
        gdn_gain = gdn_out_gain[l][None, :].astype(F32)
        mq_gain = mem_q_gain[l][None, :].astype(F32)

        p, p16, ps = _in_proj(yp, gain, w_big, w_small, min(1024, b * t))
        knt, fvt, lf, qa, ka, va = _fox_prep_prompt(p, ps, fox_q_gain[l], fox_k_gain[l], fox_f_bias[l], b, t)
        kn, fv = [z.reshape(b, H_FOX, DH_FOX, t).transpose(0, 3, 1, 2) for z in (knt, fvt)]
        o_fox = _fox_attn_prompt(qa, ka, va, b, t).reshape(b * t, FOX_W)
        gq, gk, gv, gg = _gdn_prep(p, ps, jnp.zeros((b, 8, 3 * GDN_W), F32), gdn_conv_w[l], gdn_A_log[l],
                                   gdn_dt_bias[l], b, t)
        o_gdn, s_new = _gdn(gq, gk, gv, gg, jnp.zeros((b, H_GDN, DK_GDN, DV_GDN), F32), b, t)
        mk, mv = _mem_kv(mem_prompt.reshape(b * n_mem, d), mem_norm_gain[l][None, :].astype(F32),
                         w_mem_kv[l].astype(BF16), mem_k_gain[l][None, :].astype(F32))
        o_mem = _mem_attn(p, mk.reshape(b, n_mem, MEM_W), mv.reshape(b, n_mem, MEM_W), mq_gain, b, t)
        conv_tail = p.reshape(b, t, -1)[:, t - (CONV_W - 1):, GQ * FOX_W:GQ * FOX_W + 3 * GDN_W]
        yp = _out_proj(o_fox, o_gdn, o_mem, p, p16, yp, wf, wg, wm, wo, gdn_gain)
        for lst, val in zip(outs[:7], (kn, fv, lf[:, :, :H_FOX], s_new, conv_tail,
                                       mk.reshape(b, n_mem, H_MEM, DH_MEM), mv.reshape(b, n_mem, H_MEM, DH_MEM))):
            lst.append(val)

        p, p16, ps = _in_proj(ys, gain, w_big, w_small, bd_ * td)
        qn, kn, lf, cn = _fox_sample_prep(p, ps, fox_q_gain[l], fox_k_gain[l], fox_f_bias[l], td)
        fv = p[:, FV * FOX_W:(FV + 1) * FOX_W]
        q4 = qn.reshape(bd_, td, H_FOX, DH_FOX).transpose(0, 2, 1, 3)
        qbd = (q4[:, :, :, None, :] * eye[None, :, None, :, None]).reshape(bd_, H_FOX * td, FOX_W).astype(BF16)
        cn3 = cn.reshape(bd_, td, LANES)[:, :, :H_FOX].transpose(0, 2, 1)
        cnq = jnp.broadcast_to(cn3.reshape(bd_, H_FOX * td, 1), (bd_, H_FOX * td, LANES))
        cnk = jnp.broadcast_to(cn3[:, :, None, :], (bd_, H_FOX, td, td)).reshape(bd_, H_FOX * td, td)
        cnk = jnp.pad(cnk, ((0, 0), (0, 0), (0, LANES - td)))
        pad_rows = lambda z: jnp.pad(z.reshape(bd_, td, FOX_W), ((0, 0), (0, page - td), (0, 0))).astype(BF16).transpose(0, 2, 1)
        pidx = (l * n_pool + page_table).reshape(-1).astype(jnp.int32)
        o_fox = _fox_decode(pidx, qbd, cnq, cnk, pad_rows(kn), pad_rows(fv), kc, vc, fc, n_pages, td)
        o_fox = o_fox.reshape(bd_ * td, FOX_W)

        init = jnp.pad(state_gdn_conv[l].astype(F32), ((0, 0), (8 - (CONV_W - 1), 0), (0, 0)))
        gq, gk, gv, gg = _gdn_prep(p, ps, init, gdn_conv_w[l], gdn_A_log[l], gdn_dt_bias[l], bd_, td)
        padc = lambda z: jnp.pad(z.reshape(bd_, td, -1), ((0, 0), (0, c - td), (0, 0))).reshape(bd_ * c, -1)
        o_gdn, s_new = _gdn(padc(gq), padc(gk), padc(gv), padc(gg), state_gdn[l].astype(F32), bd_, c)
        o_gdn = o_gdn.reshape(bd_, c, GDN_W)[:, :td].reshape(bd_ * td, GDN_W)
        o_mem = _mem_attn(p, cache_mem_k[l].reshape(bd_, n_mem, MEM_W), cache_mem_v[l].reshape(bd_, n_mem, MEM_W),
                          mq_gain, bd_, td)
        conv_tail = jnp.concatenate([state_gdn_conv[l].astype(F32),
                                     p.reshape(bd_, td, -1)[:, :, GQ * FOX_W:GQ * FOX_W + 3 * GDN_W]],
                                    axis=1)[:, td:]
        ys = _out_proj(o_fox, o_gdn, o_mem, p, p16, ys, wf, wg, wm, wo, gdn_gain)
        for lst, val in zip(outs[7:], (kn.reshape(bd_, td, H_FOX, DH_FOX), fv.reshape(bd_, td, H_FOX, DH_FOX),
                                       lf.reshape(bd_, td, LANES)[:, :, :H_FOX], s_new, conv_tail)):
            lst.append(val)

    return (yp.reshape(b, t, d), ys.reshape(bd_, td, d)) + tuple(jnp.stack(z) for z in outs)
```

```python
import functools

import numpy as np
import jax
import jax.numpy as jnp
from jax import lax
from jax.experimental import pallas as pl
from jax.experimental.pallas import tpu as pltpu

F32, BF16 = jnp.float32, jnp.bfloat16

H_FOX, DH_FOX = 8, 64
FOX_W = H_FOX * DH_FOX
H_GDN, DK_GDN, DV_GDN = 4, 128, 128
GDN_W = H_GDN * DV_GDN
CONV_W = 4
GDN_CHUNK = 64
H_MEM, DH_MEM = 4, 128
MEM_W = H_MEM * DH_MEM
N_BRANCH = 3
EPS = 1e-6
NEG = -1e30
LANES = 128
VMEM_LIMIT = 56 * 1024 * 1024
DECODE_PAGES_PER_STEP = 32

FQ, FK, FV, GQ, GK, GV, MQ, FZ = range(8)
GZ, MZ = 0, 1
GATE0 = 1
P_HALF = 8 * FOX_W
LOG2E = 1.4426950408889634
SM_FF, SM_GA, SM_GB = 0, 8, 12


def _params(*sem):
    return pltpu.CompilerParams(dimension_semantics=sem, vmem_limit_bytes=VMEM_LIMIT)


def _dot(a, b):
    return jnp.dot(a, b, preferred_element_type=F32)


def _dot_nt(a, b):
    return lax.dot_general(a, b, (((1,), (1,)), ((), ())), preferred_element_type=F32)


def _dot_tn(a, b):
    return lax.dot_general(a, b, (((0,), (0,)), ((), ())), preferred_element_type=F32)


def _split2(x):
    hi = x.astype(BF16)
    lo = (x - hi.astype(F32)).astype(BF16)
    return hi, lo


def _split3(x):
    hi = x.astype(BF16)
    r = x - hi.astype(F32)
    mid = r.astype(BF16)
    lo = (r - mid.astype(F32)).astype(BF16)
    return hi, mid, lo


def _dot3_lhs_exact(m, x):
    hi, mid, lo = _split3(x)
    return _dot(m, hi) + _dot(m, mid) + _dot(m, lo)


def _dot3_rhs_exact(x, m):
    hi, mid, lo = _split3(x)
    return _dot(hi, m) + _dot(mid, m) + _dot(lo, m)


def _sigmoid(x):
    return 0.5 * jnp.tanh(0.5 * x) + 0.5


def _silu(x):
    return x * _sigmoid(x)


def _softplus(x):
    return jnp.maximum(x, 0.0) + jnp.log1p(jnp.exp(-jnp.abs(x)))


def _log_sigmoid(x):
    return jnp.minimum(x, 0.0) - jnp.log1p(jnp.exp(-jnp.abs(x)))


def _lane_bcast(x, lane):
    return jnp.broadcast_to(x[:, lane:lane + 1], x.shape)


def _headnorm128(x, gain, nheads):
    outs = []
    for h in range(nheads):
        seg = x[:, h * LANES:(h + 1) * LANES]
        ms = jnp.mean(seg * seg, axis=-1, keepdims=True)
        outs.append(seg * lax.rsqrt(ms + EPS) * gain)
    return jnp.concatenate(outs, axis=1)


def _in_proj_kernel(x_ref, g_ref, w_ref, ws_ref, p32_ref, p16_ref, ps_ref, h_ref, *, n32):
    j = pl.program_id(1)

    @pl.when(j == 0)
    def _():
        x = x_ref[...]
        ms = jnp.mean(x * x, axis=-1, keepdims=True)
        h = (x * lax.rsqrt(ms + EPS) * g_ref[...]).astype(BF16)
        h_ref[...] = h
        ps_ref[...] = _dot(h, ws_ref[...])

    res = _dot(h_ref[...], w_ref[j])

    @pl.when(j < n32)
    def _():
        p32_ref[...] = res

    @pl.when(j >= n32)
    def _():
        p16_ref[...] = res.astype(BF16)


def _in_proj(x, gain, w_big, w_small, tm):
    n, d = x.shape
    ntile, _, tn = w_big.shape
    n32 = P_HALF // tn
    n16 = ntile - n32
    return pl.pallas_call(
        functools.partial(_in_proj_kernel, n32=n32),
        grid=(n // tm, ntile),
        in_specs=[pl.BlockSpec((tm, d), lambda i, j: (i, 0)),
                  pl.BlockSpec((1, d), lambda i, j: (0, 0)),
                  pl.BlockSpec((ntile, d, tn), lambda i, j: (0, 0, 0), pipeline_mode=pl.Buffered(1)),
                  pl.BlockSpec((d, LANES), lambda i, j: (0, 0))],
        out_specs=[pl.BlockSpec((tm, tn), lambda i, j: (i, jnp.minimum(j, n32 - 1))),
                   pl.BlockSpec((tm, tn), lambda i, j: (i, jnp.maximum(j - n32, 0))),
                   pl.BlockSpec((tm, LANES), lambda i, j: (i, 0))],
        out_shape=[jax.ShapeDtypeStruct((n, P_HALF), F32), jax.ShapeDtypeStruct((n, n16 * tn), BF16),
                   jax.ShapeDtypeStruct((n, LANES), F32)],
        scratch_shapes=[pltpu.VMEM((tm, d), BF16)],
        compiler_params=_params("parallel", "arbitrary"),
        name="in_proj")(x, gain, w_big, w_small)


N_EXT = 3


def _fox_headnorm(x, gain, bd):
    ss = _dot((x * x).astype(BF16), bd)
    return x * lax.rsqrt(ss * (1.0 / DH_FOX) + EPS) * gain


def _fox_prep_kernel(fq_ref, fk_ref, fv_ref, ps_ref, qg_ref, kg_ref, fb_ref, bd_ref, tri_ref, sele_ref, selv_ref,
                     knt_ref, vt32_ref, lf_ref, qa_ref, ka_ref, va_ref, carry_ref):
    @pl.when(pl.program_id(1) == 0)
    def _():
        carry_ref[...] = jnp.zeros_like(carry_ref)

    bd = bd_ref[...]
    qn = _fox_headnorm(fq_ref[...], qg_ref[...], bd) * (DH_FOX ** -0.5 * LOG2E)
    kn = _fox_headnorm(fk_ref[...], kg_ref[...], bd)
    fv = fv_ref[...]
    knt_ref[0] = kn.T
    vt32_ref[0] = fv.T
    lf = _log_sigmoid(ps_ref[...] + fb_ref[...])
    lf_ref[0] = lf
    c = _dot3_lhs_exact(tri_ref[...], lf) + carry_ref[0:1, :]
    tm = c.shape[0]
    carry_ref[...] = jnp.broadcast_to(c[tm - 1:tm, :], carry_ref.shape)
    ext = _dot(jnp.concatenate(_split3(c * LOG2E), axis=1), sele_ref[...])
    lane = lax.broadcasted_iota(jnp.int32, (tm, LANES), 1)
    in_k_ext = (lane >= DH_FOX + N_EXT) & (lane < DH_FOX + 2 * N_EXT)
    q_ones = in_k_ext.astype(F32)
    k_ones = ((lane >= DH_FOX) & (lane < DH_FOX + N_EXT)).astype(F32)
    for h in range(H_FOX):
        pair = slice((h // 2) * LANES, (h // 2 + 1) * LANES)
        bq, bk = qn[:, pair], kn[:, pair]
        if h % 2:
            bq, bk = pltpu.roll(bq, DH_FOX, 1), pltpu.roll(bk, DH_FOX, 1)
        eq = pltpu.roll(ext[:, 0:LANES], DH_FOX - 8 * h, 1)
        ek = pltpu.roll(ext[:, LANES:2 * LANES], DH_FOX - 8 * h, 1)
        cols = slice(h * LANES, (h + 1) * LANES)
        qa_ref[0, :, cols] = jnp.where(lane < DH_FOX, bq, jnp.where(lane < DH_FOX + N_EXT, eq, q_ones)).astype(BF16)
        ka_ref[0, :, cols] = jnp.where(lane < DH_FOX, bk, jnp.where(in_k_ext, ek, k_ones)).astype(BF16)
    vt = _dot_nt(selv_ref[...], fv.astype(BF16))
    r = lax.broadcasted_iota(jnp.int32, vt.shape, 0) & (2 * LANES - 1)
    va_ref[0, 0] = jnp.where((r == DH_FOX) | (r == LANES), 1.0, vt).astype(BF16)


def _fox_consts(tm):
    a = np.arange(FOX_W)
    bd = (a[:, None] // DH_FOX == a[None, :] // DH_FOX).astype(np.float32)
    r = np.arange(tm)
    tri = (r[None, :] <= r[:, None]).astype(np.float32)
    aw = H_FOX * LANES
    sele = np.zeros((N_EXT * LANES, 2 * LANES), np.float32)
    selv = np.zeros((aw, FOX_W), np.float32)
    for h in range(H_FOX):
        for d in range(DH_FOX):
            selv[h * LANES + (h % 2) * DH_FOX + d, h * DH_FOX + d] = 1.0
        for e in range(N_EXT):
            sele[e * LANES + h, 8 * h + e] = 1.0
            sele[e * LANES + h, LANES + 8 * h + N_EXT + e] = -1.0
    bf = lambda z: jnp.asarray(z, BF16)
    return bf(bd), bf(tri), bf(sele), bf(selv)


def _tile_gain(g, reps):
    return jnp.tile(g.astype(F32), reps)[None, :]


def _pad_row(vals, offset):
    row = jnp.zeros((LANES,), F32)
    return lax.dynamic_update_slice(row, vals.astype(F32), (offset,))[None, :]


def _fox_prep_prompt(p, ps, qgain, kgain, fbias, b, t):
    tm = min(512, t)
    nt = t // tm
    bd, tri, sele, selv = _fox_consts(tm)
    aw = H_FOX * LANES
    row = lambda blk: pl.BlockSpec((tm, FOX_W), lambda i, j, blk=blk: (i * nt + j, blk))
    const = lambda shape: pl.BlockSpec(shape, lambda i, j: (0,) * len(shape))
    tok = lambda w: pl.BlockSpec((1, tm, w), lambda i, j: (i, j, 0))
    tr = pl.BlockSpec((1, FOX_W, tm), lambda i, j: (i, 0, j))
    return pl.pallas_call(
        _fox_prep_kernel,
        grid=(b, nt),
        in_specs=[row(FQ), row(FK), row(FV),
                  pl.BlockSpec((tm, LANES), lambda i, j: (i * nt + j, 0)),
                  const((1, FOX_W)), const((1, FOX_W)), const((1, LANES)),
                  const(bd.shape), const(tri.shape), const(sele.shape), const(selv.shape)],
        out_specs=[tr, tr, tok(LANES), tok(aw), tok(aw),
                   pl.BlockSpec((1, 1, aw, tm), lambda i, j: (i, j, 0, 0))],
        out_shape=[jax.ShapeDtypeStruct((b, FOX_W, t), F32), jax.ShapeDtypeStruct((b, FOX_W, t), F32),
                   jax.ShapeDtypeStruct((b, t, LANES), F32),
                   jax.ShapeDtypeStruct((b, t, aw), BF16), jax.ShapeDtypeStruct((b, t, aw), BF16),
                   jax.ShapeDtypeStruct((b, nt, aw, tm), BF16)],
        scratch_shapes=[pltpu.VMEM((8, LANES), F32)],
        compiler_params=_params("parallel", "arbitrary"),
        name="fox_prep")(p, p, p, ps, _tile_gain(qgain, H_FOX), _tile_gain(kgain, H_FOX), _pad_row(fbias, SM_FF),
                         bd, tri, sele, selv)


def _fox_attn_kernel(q_ref, k_ref, vt_ref, o_ref, m_ref, acc_ref, sa_ref, sb_ref, *, tk):
    qi = pl.program_id(2)
    tq = q_ref.shape[1]
    heads = [slice(hh * LANES, (hh + 1) * LANES) for hh in range(2)]
    m_ref[...] = jnp.full(m_ref.shape, NEG, F32)
    acc_ref[...] = jnp.zeros_like(acc_ref)

    def scores(ki, hh):
        r0 = pl.multiple_of(ki * tk, tk)
        return _dot_nt(k_ref[0, pl.ds(r0, tk), heads[hh]], q_ref[0, :, heads[hh]])

    def accumulate(st, ki, hh):
        m_prev = m_ref[hh]
        m_new = jnp.maximum(m_prev, jnp.max(st, axis=0, keepdims=True))
        pt = jnp.exp2(st - m_new).astype(BF16)
        acc_ref[hh] = jnp.exp2(m_prev - m_new) * acc_ref[hh] + _dot(vt_ref[0, ki, heads[hh], :], pt)
        m_ref[hh] = m_new

    def causal(st):
        krow = lax.broadcasted_iota(jnp.int32, (tk, tq), 0)
        qcol = lax.broadcasted_iota(jnp.int32, (tk, tq), 1)
        return jnp.where(krow <= qcol, st, NEG)

    for hh in range(2):
        sa_ref[hh] = scores(0, hh)

    def pair(j, carry):
        for hh in range(2):
            sb_ref[hh] = scores(2 * j + 1, hh)
            accumulate(sa_ref[hh], 2 * j, hh)
        for hh in range(2):
            sa_ref[hh] = scores(2 * j + 2, hh)
            accumulate(sb_ref[hh], 2 * j + 1, hh)
        return carry

    lax.fori_loop(0, qi // 2, pair, 0)

    @pl.when(qi % 2 == 1)
    def _():
        for hh in range(2):
            sb_ref[hh] = scores(qi, hh)
            accumulate(sa_ref[hh], qi - 1, hh)
        for hh in range(2):
            accumulate(causal(sb_ref[hh]), qi, hh)

    @pl.when(qi % 2 == 0)
    def _():
        for hh in range(2):
            accumulate(causal(sa_ref[hh]), qi, hh)

    a0, a1 = acc_ref[0], acc_ref[1]
    row = lax.broadcasted_iota(jnp.int32, a0.shape, 0)
    ot = jnp.where(row < DH_FOX, a0 / a0[DH_FOX:DH_FOX + 1, :], a1 / a1[0:1, :])
    o_ref[0] = ot.T


def _fox_attn_prompt(qa, ka, vat, b, t):
    tq = vat.shape[3]
    pair = 2 * LANES
    return pl.pallas_call(
        functools.partial(_fox_attn_kernel, tk=tq),
        grid=(b, H_FOX // 2, t // tq),
        in_specs=[pl.BlockSpec((1, tq, pair), lambda i, h, q: (i, q, h)),
                  pl.BlockSpec((1, t, pair), lambda i, h, q: (i, 0, h)),
                  pl.BlockSpec((1, t // tq, pair, tq), lambda i, h, q: (i, 0, h, 0))],
        out_specs=pl.BlockSpec((1, tq, LANES), lambda i, h, q: (i, q, h)),
        out_shape=jax.ShapeDtypeStruct((b, t, FOX_W), F32),
        scratch_shapes=[pltpu.VMEM((2, 1, tq), F32), pltpu.VMEM((2, LANES, tq), F32),
                        pltpu.VMEM((2, tq, tq), F32), pltpu.VMEM((2, tq, tq), F32)],
        compiler_params=_params("parallel", "parallel", "arbitrary"),
        name="fox_attn")(qa, ka, vat)


def _fox_sample_prep_kernel(fq_ref, fk_ref, ps_ref, qg_ref, kg_ref, fb_ref, bd_ref, tri_ref,
                            qn_ref, kn_ref, lf_ref, cn_ref):
    bd = bd_ref[...]
    qn_ref[...] = _fox_headnorm(fq_ref[...], qg_ref[...], bd) * (DH_FOX ** -0.5)
    kn_ref[...] = _fox_headnorm(fk_ref[...], kg_ref[...], bd)
    lf = _log_sigmoid(ps_ref[...] + fb_ref[...])
    lf_ref[...] = lf
    cn_ref[...] = _dot3_lhs_exact(tri_ref[...], lf)


def _fox_sample_prep(p, ps, qgain, kgain, fbias, td):
    n = p.shape[0]
    a = np.arange(FOX_W)
    bd = jnp.asarray((a[:, None] // DH_FOX == a[None, :] // DH_FOX), BF16)
    r = np.arange(n)
    tri = jnp.asarray((r[None, :] <= r[:, None]) & (r[None, :] // td == r[:, None] // td), BF16)
    full = lambda shape: pl.BlockSpec(shape, lambda i: (0,) * len(shape))
    return pl.pallas_call(
        _fox_sample_prep_kernel,
        grid=(1,),
        in_specs=[pl.BlockSpec((n, FOX_W), lambda i: (0, FQ)), pl.BlockSpec((n, FOX_W), lambda i: (0, FK)),
                  full((n, LANES)), full((1, FOX_W)), full((1, FOX_W)), full((1, LANES)),
                  full(bd.shape), full(tri.shape)],
        out_specs=[full((n, FOX_W)), full((n, FOX_W)), full((n, LANES)), full((n, LANES))],
        out_shape=[jax.ShapeDtypeStruct((n, FOX_W), F32), jax.ShapeDtypeStruct((n, FOX_W), F32),
                   jax.ShapeDtypeStruct((n, LANES), F32), jax.ShapeDtypeStruct((n, LANES), F32)],
        compiler_params=_params("arbitrary"),
        name="fox_sample_prep")(p, p, ps, _tile_gain(qgain, H_FOX), _tile_gain(kgain, H_FOX),
                                _pad_row(fbias, SM_FF), bd, tri)


def _fox_decode_kernel(pidx_ref, q_ref, cn_ref, cnk_ref, knew_ref, vnew_ref, uo_ref, *rest, g, td):
    k_refs, v_refs, f_refs = rest[0:g], rest[g:2 * g], rest[2 * g:3 * g]
    o_ref, m_ref, l_ref, acc_ref, suf_ref = rest[3 * g:]
    p = pl.program_id(1)
    q = q_ref[0]
    cn = cn_ref[0]
    nrow = q.shape[0]

    @pl.when(p == 0)
    def _():
        s = _dot(q, knew_ref[0])
        qpos = lax.broadcasted_iota(jnp.int32, s.shape, 0) % td
        kpos = lax.broadcasted_iota(jnp.int32, s.shape, 1)
        s = jnp.where(kpos <= qpos, s + cn - cnk_ref[0], NEG)
        m = jnp.max(s, axis=-1, keepdims=True)
        pr = jnp.exp(s - m)
        m_ref[...] = m
        l_ref[...] = jnp.sum(pr, axis=-1, keepdims=True)
        acc_ref[...] = _dot_nt(pr.astype(BF16), vnew_ref[0])
        suf_ref[...] = jnp.zeros_like(suf_ref)

    lf = jnp.concatenate([f_refs[j][0] for j in range(g)], axis=0)
    rs = _dot3_rhs_exact(lf, uo_ref[...])
    suf = suf_ref[...]
    scores = []
    for j in range(g):
        z = rs[j * H_FOX:(j + 1) * H_FOX, 0:LANES] + suf
        bias = jnp.concatenate([jnp.broadcast_to(z[h:h + 1, :], (td, LANES)) for h in range(H_FOX)], axis=0)
        scores.append(_dot(q, k_refs[j][0].astype(BF16)) + bias + cn)
        suf = suf + rs[j * H_FOX:(j + 1) * H_FOX, LANES:]
    suf_ref[...] = suf
    m_prev = m_ref[...]
    m_new = m_prev
    for s in scores:
        m_new = jnp.maximum(m_new, jnp.max(s, axis=-1, keepdims=True))
    alpha = jnp.exp(m_prev - m_new)
    l = alpha * l_ref[...]
    acc = alpha * acc_ref[...]
    for j in range(g):
        pr = jnp.exp(scores[j] - m_new)
        l = l + jnp.sum(pr, axis=-1, keepdims=True)
        acc = acc + _dot_nt(pr.astype(BF16), v_refs[j][0].astype(BF16))
    m_ref[...] = m_new
    l_ref[...] = l
    acc_ref[...] = acc

    @pl.when(p == pl.num_programs(1) - 1)
    def _():
        o = acc / l
        head = lax.broadcasted_iota(jnp.int32, (td, FOX_W), 1) // DH_FOX
        out = jnp.zeros((td, FOX_W), F32)
        for h in range(H_FOX):
            out = out + jnp.where(head == h, o[h * td:(h + 1) * td, :], 0.0)
        o_ref[0] = out


def _fox_decode(pidx, qbd, cn, cnk, knew, vnew, kc, vc, fc, n_pages, td):
    bd_, nrow, _ = qbd.shape
    page = kc.shape[2]
    g = min(DECODE_PAGES_PER_STEP, n_pages)
    steps = n_pages // g
    j = np.arange(page)
    uo = jnp.asarray(np.concatenate([j[:, None] > j[None, :], np.ones((page, page), bool)], axis=1), BF16)

    def page_map(j):
        return lambda b, p, idx: (idx[b * n_pages + (n_pages - 1 - (p * g + j))], 0, 0)

    per_b = lambda shape: pl.BlockSpec((1,) + shape, lambda b, p, idx: (b, 0, 0))
    in_specs = [per_b((nrow, FOX_W)), per_b((nrow, LANES)), per_b((nrow, LANES)),
                per_b((FOX_W, page)), per_b((FOX_W, page)),
                pl.BlockSpec(uo.shape, lambda b, p, idx: (0, 0))]
    in_specs += [pl.BlockSpec((1, FOX_W, page), page_map(j)) for j in range(g)]
    in_specs += [pl.BlockSpec((1, FOX_W, page), page_map(j)) for j in range(g)]
    in_specs += [pl.BlockSpec((1, H_FOX, page), page_map(j)) for j in range(g)]
    grid_spec = pltpu.PrefetchScalarGridSpec(
        num_scalar_prefetch=1, grid=(bd_, steps), in_specs=in_specs,
        out_specs=pl.BlockSpec((1, td, FOX_W), lambda b, p, idx: (b, 0, 0)),
        scratch_shapes=[pltpu.VMEM((nrow, 1), F32), pltpu.VMEM((nrow, 1), F32), pltpu.VMEM((nrow, FOX_W), F32),
                        pltpu.VMEM((H_FOX, LANES), F32)])
    return pl.pallas_call(
        functools.partial(_fox_decode_kernel, g=g, td=td),
        grid_spec=grid_spec,
        out_shape=jax.ShapeDtypeStruct((bd_, td, FOX_W), F32),
        compiler_params=_params("parallel", "arbitrary"),
        name="fox_decode")(pidx, qbd, cn, cnk, knew, vnew, uo, *([kc] * g), *([vc] * g), *([fc] * g))


def _gdn_prep_kernel(gq_ref, gk_ref, gv_ref, ps_ref, init_ref, w_ref, alog_ref, dtb_ref,
                     q_ref, k_ref, v_ref, g_ref, ext_ref):
    tm = gq_ref.shape[0]

    @pl.when(pl.program_id(1) == 0)
    def _():
        ext_ref[0:8, :] = init_ref[0]

    @pl.when(pl.program_id(1) > 0)
    def _():
        ext_ref[0:8, :] = ext_ref[tm:tm + 8, :]

    ext_ref[8:8 + tm, 0:GDN_W] = gq_ref[...]
    ext_ref[8:8 + tm, GDN_W:2 * GDN_W] = gk_ref[...]
    ext_ref[8:8 + tm, 2 * GDN_W:3 * GDN_W] = gv_ref[...]
    conv = ext_ref[8:8 + tm, :] * w_ref[CONV_W - 1:CONV_W, :]
    for i in range(CONV_W - 1):
        off = 8 - (CONV_W - 1) + i
        conv = conv + ext_ref[off:off + tm, :] * w_ref[i:i + 1, :]
    s = _silu(conv)

    def l2(x):
        outs = []
        for h in range(H_GDN):
            seg = x[:, h * LANES:(h + 1) * LANES]
            outs.append(seg * lax.rsqrt(jnp.sum(seg * seg, axis=-1, keepdims=True) + EPS))
        return jnp.concatenate(outs, axis=1)

    q_ref[...] = l2(s[:, 0:GDN_W]) * (DK_GDN ** -0.5)
    k_ref[...] = l2(s[:, GDN_W:2 * GDN_W])
    v_ref[...] = s[:, 2 * GDN_W:3 * GDN_W]
    ps = ps_ref[...]
    lane = lax.broadcasted_iota(jnp.int32, ps.shape, 1)
    gate = -jnp.exp(alog_ref[...]) * _softplus(ps + dtb_ref[...])
    beta = _sigmoid(ps)
    g_ref[...] = jnp.where((lane >= SM_GA) & (lane < SM_GB), gate,
                           jnp.where((lane >= SM_GB) & (lane < SM_GB + H_GDN), beta, 0.0))


def _gdn_prep(p, ps, init, conv_w, a_log, dt_bias, b, t):
    tm = min(512, t)
    nt = t // tm
    n = b * t
    row = lambda blk: pl.BlockSpec((tm, GDN_W), lambda i, j, blk=blk: (i * nt + j, blk))
    out = pl.BlockSpec((tm, GDN_W), lambda i, j: (i * nt + j, 0))
    small = pl.BlockSpec((tm, LANES), lambda i, j: (i * nt + j, 0))
    const = lambda shape: pl.BlockSpec(shape, lambda i, j: (0,) * len(shape))
    return pl.pallas_call(
        _gdn_prep_kernel,
        grid=(b, nt),
        in_specs=[row(GQ), row(GK), row(GV), small,
                  pl.BlockSpec((1, 8, 3 * GDN_W), lambda i, j: (i, 0, 0)),
                  const((CONV_W, 3 * GDN_W)), const((1, LANES)), const((1, LANES))],
        out_specs=[out, out, out, small],
        out_shape=[jax.ShapeDtypeStruct((n, GDN_W), F32)] * 3 + [jax.ShapeDtypeStruct((n, LANES), F32)],
        scratch_shapes=[pltpu.VMEM((tm + 8, 3 * GDN_W), F32)],
        compiler_params=_params("parallel", "arbitrary"),
        name="gdn_prep")(p, p, p, ps, init, conv_w.astype(F32), _pad_row(a_log, SM_GA), _pad_row(dt_bias, SM_GA))


def _gdn_chunk_kernel(q_ref, k_ref, v_ref, g_ref, tri_ref,
                      m_ref, qk_ref, u_ref, w_ref, qg_ref, kg_ref, egl_ref, *, nch):
    c = GDN_CHUNK
    tri = tri_ref[...]
    ri = lax.broadcasted_iota(jnp.int32, (c, c), 0)
    ci = lax.broadcasted_iota(jnp.int32, (c, c), 1)
    lane = lax.broadcasted_iota(jnp.int32, (c, LANES), 1)
    for cc in range(nch):
        rows = slice(cc * c, (cc + 1) * c)
        gt = g_ref[rows, :]
        gcum = _dot3_lhs_exact(tri, gt)
        for h in range(H_GDN):
            cols = slice(h * LANES, (h + 1) * LANES)
            gcol = _lane_bcast(gcum, SM_GA + h)
            beta = _lane_bcast(gt, SM_GB + h)
            hi, mid, lo = [z.astype(F32) for z in _split3(gcol)]
            a = jnp.where(lane == 0, hi, jnp.where(lane == 1, mid, jnp.where(lane == 2, lo,
                          jnp.where(lane < 6, 1.0, 0.0))))
            bm = jnp.where(lane < 3, 1.0, jnp.where(lane == 3, -hi, jnp.where(lane == 4, -mid,
                           jnp.where(lane == 5, -lo, 0.0))))
            diff = _dot_nt(a.astype(BF16), bm.astype(BF16))
            decay = jnp.exp(jnp.where(ci <= ri, diff, NEG))
            qh, kh, vh = q_ref[rows, cols], k_ref[rows, cols], v_ref[rows, cols]
            kb = kh * beta
            khb = kh.astype(BF16)
            m_ref[cc * H_GDN + h] = jnp.where(ci < ri, _dot_nt(kb.astype(BF16), khb) * decay, 0.0)
            qk_ref[cc * H_GDN + h] = _dot_nt(qh.astype(BF16), khb) * decay
            eg = jnp.exp(gcol)
            glast = gcol[c - 1:c, :]
            u_ref[rows, cols] = vh * beta
            w_ref[rows, cols] = kb * eg
            qg_ref[rows, cols] = (qh * eg).astype(BF16)
            kg_ref[rows, cols] = (kh * jnp.exp(glast - gcol)).astype(BF16)
            egl_ref[cc * H_GDN + h] = jnp.broadcast_to(jnp.exp(glast), (8, LANES))


def _gdn_chunk(q, k, v, g):
    n = q.shape[0]
    c = GDN_CHUNK
    tm = min(256, n)
    nch = tm // c
    r = np.arange(c)
    tri = jnp.asarray(r[None, :] <= r[:, None], BF16)
    nc = n // c * H_GDN
    row = pl.BlockSpec((tm, GDN_W), lambda i: (i, 0))
    blk = lambda shape: pl.BlockSpec((nch * H_GDN,) + shape, lambda i: (i, 0, 0))
    return pl.pallas_call(
        functools.partial(_gdn_chunk_kernel, nch=nch),
        grid=(n // tm,),
        in_specs=[row, row, row, pl.BlockSpec((tm, LANES), lambda i: (i, 0)), pl.BlockSpec((c, c), lambda i: (0, 0))],
        out_specs=[blk((c, c)), blk((c, c)), row, row, row, row, blk((8, LANES))],
        out_shape=[jax.ShapeDtypeStruct((nc, c, c), F32), jax.ShapeDtypeStruct((nc, c, c), F32)]
                  + [jax.ShapeDtypeStruct((n, GDN_W), F32)] * 2 + [jax.ShapeDtypeStruct((n, GDN_W), BF16)] * 2
                  + [jax.ShapeDtypeStruct((nc, 8, LANES), F32)],
        compiler_params=_params("parallel"),
        name="gdn_chunk")(q, k, v, g, tri)


def _gdn_inv_kernel(m_ref, x_ref):
    c = GDN_CHUNK
    for i in range(c):
        ext = 8 * (i // 8 + 1)
        e_i = (lax.broadcasted_iota(jnp.int32, (ext, LANES), 0) == i).astype(F32)

        def body(j, acc, i=i, ext=ext):
            return acc - m_ref[i, pl.ds(j, 1), :] * x_ref[j, 0:ext, :]

        x_ref[i, 0:ext, :] = lax.fori_loop(0, i, body, e_i, unroll=min(max(i, 1), 8))
        if ext < c:
            x_ref[i, ext:c, :] = jnp.zeros((c - ext, LANES), F32)


def _gdn_inv(mt):
    c = GDN_CHUNK
    nc = mt.shape[2]
    spec = pl.BlockSpec((c, c, LANES), lambda i: (0, 0, i))
    return pl.pallas_call(
        _gdn_inv_kernel,
        grid=(nc // LANES,),
        in_specs=[spec], out_specs=spec,
        out_shape=jax.ShapeDtypeStruct((c, c, nc), F32),
        compiler_params=_params("parallel"),
        name="gdn_inv")(mt)


def _gdn_scan_kernel(x_ref, qk_ref, u_ref, w_ref, qg_ref, kg_ref, egl_ref, s0_ref, o_ref, sout_ref,
                     s_ref, us_ref, wq_ref, *, nch):
    c = GDN_CHUNK

    @pl.when(pl.program_id(1) == 0)
    def _():
        s_ref[...] = s0_ref[0]

    heads = [slice(h * LANES, (h + 1) * LANES) for h in range(H_GDN)]

    def rows(cc):
        return cc * c if isinstance(cc, int) else pl.multiple_of(cc * c, c)

    def solve(cc, slot):
        r0 = rows(cc)
        for h in range(H_GDN):
            rhs = jnp.concatenate([u_ref[pl.ds(r0, c), heads[h]], w_ref[pl.ds(r0, c), heads[h]]], axis=1)
            xh, xl = _split2(x_ref[cc * H_GDN + h])
            rh, rl = _split2(rhs)
            sol = _dot(xh, rh) + _dot(xh, rl) + _dot(xl, rh)
            us_ref[slot, :, heads[h]] = sol[:, 0:DV_GDN]
            wq_ref[slot, 0:c, heads[h]] = sol[:, DV_GDN:].astype(BF16)
            wq_ref[slot, c:2 * c, heads[h]] = qg_ref[pl.ds(r0, c), heads[h]].astype(BF16)

    def advance(cc, slot):
        r0 = rows(cc)
        sts = [s_ref[h] for h in range(H_GDN)]
        boths = [_dot(wq_ref[slot, :, heads[h]], sts[h].astype(BF16)) for h in range(H_GDN)]
        dbs = [(us_ref[slot, :, heads[h]] - boths[h][0:c]).astype(BF16) for h in range(H_GDN)]
        for h in range(H_GDN):
            egl = jnp.broadcast_to(egl_ref[cc * H_GDN + h][0:1, :], (DK_GDN, DV_GDN))
            s_ref[h] = sts[h] * egl + _dot_tn(kg_ref[pl.ds(r0, c), heads[h]].astype(BF16), dbs[h])
        for h in range(H_GDN):
            o_ref[pl.ds(r0, c), heads[h]] = boths[h][c:] + _dot(qk_ref[cc * H_GDN + h].astype(BF16), dbs[h])

    solve(0, 0)

    def pair(j, carry):
        solve(2 * j + 1, 1)
        advance(2 * j, 0)
        solve(jnp.minimum(2 * j + 2, nch - 1), 0)
        advance(2 * j + 1, 1)
        return carry

    lax.fori_loop(0, nch // 2, pair, 0)
    if nch % 2:
        advance(nch - 1, 0)

    @pl.when(pl.program_id(1) == pl.num_programs(1) - 1)
    def _():
        sout_ref[0] = s_ref[...]


def _gdn_scan(x, qk, u, w, qg, kg, egl, s0, b, t):
    c = GDN_CHUNK
    tg = min(1024, t)
    ng = t // tg
    nch = tg // c
    row = pl.BlockSpec((tg, GDN_W), lambda i, j: (i * ng + j, 0))
    blk = lambda shape: pl.BlockSpec((nch * H_GDN,) + shape, lambda i, j: (i * ng + j, 0, 0))
    st = pl.BlockSpec((1, H_GDN, DK_GDN, DV_GDN), lambda i, j: (i, 0, 0, 0))
    return pl.pallas_call(
        functools.partial(_gdn_scan_kernel, nch=nch),
        grid=(b, ng),
        in_specs=[blk((c, c)), blk((c, c)), row, row, row, row, blk((8, LANES)), st],
        out_specs=[row, st],
        out_shape=[jax.ShapeDtypeStruct((b * t, GDN_W), F32), jax.ShapeDtypeStruct((b, H_GDN, DK_GDN, DV_GDN), F32)],
        scratch_shapes=[pltpu.VMEM((H_GDN, DK_GDN, DV_GDN), F32), pltpu.VMEM((2, c, GDN_W), F32),
                        pltpu.VMEM((2, 2 * c, GDN_W), BF16)],
        compiler_params=_params("parallel", "arbitrary"),
        name="gdn_scan")(x, qk, u, w, qg, kg, egl, s0)


def _gdn(q, k, v, g, s0, b, t):
    c = GDN_CHUNK
    m, qk, u, w, qg, kg, egl = _gdn_chunk(q, k, v, g)
    nc = m.shape[0]
    pad = (-nc) % LANES
    mt = jnp.pad(m.reshape(nc, c * c), ((0, pad), (0, 0))).T.reshape(c, c, nc + pad)
    x = _gdn_inv(mt).reshape(c * c, nc + pad).T[:nc].reshape(nc, c, c)
    return _gdn_scan(x, qk, u, w, qg, kg, egl, s0, b, t)


def _mem_kv_kernel(m_ref, g_ref, w_ref, kg_ref, mk_ref, mv_ref):
    x = m_ref[...]
    ms = jnp.mean(x * x, axis=-1, keepdims=True)
    h = (x * lax.rsqrt(ms + EPS) * g_ref[...]).astype(BF16)
    kv = _dot(h, w_ref[...])
    mk_ref[...] = _headnorm128(kv[:, 0:MEM_W], kg_ref[...], H_MEM)
    mv_ref[...] = kv[:, MEM_W:]


def _mem_kv(mem, gain, w_kv, k_gain):
    n, d = mem.shape
    tm = 256
    return pl.pallas_call(
        _mem_kv_kernel,
        grid=(n // tm,),
        in_specs=[pl.BlockSpec((tm, d), lambda i: (i, 0)), pl.BlockSpec((1, d), lambda i: (0, 0)),
                  pl.BlockSpec((d, 2 * MEM_W), lambda i: (0, 0)), pl.BlockSpec((1, LANES), lambda i: (0, 0))],
        out_specs=[pl.BlockSpec((tm, MEM_W), lambda i: (i, 0))] * 2,
        out_shape=[jax.ShapeDtypeStruct((n, MEM_W), F32)] * 2,
        compiler_params=_params("parallel"),
        name="mem_kv")(mem, gain, w_kv, k_gain)


def _mem_attn_kernel(q_ref, mk_ref, mv_ref, g_ref, o_ref, *, cast, nb):
    dt = BF16 if cast else F32
    tm = q_ref.shape[0] // nb
    qn_all = _headnorm128(q_ref[...], g_ref[...], H_MEM)
    for bb in range(nb):
        qn = qn_all[bb * tm:(bb + 1) * tm]
        outs = []
        for h in range(H_MEM):
            cols = slice(h * LANES, (h + 1) * LANES)
            s = _dot_nt(qn[:, cols].astype(dt), mk_ref[bb, :, cols].astype(dt)) * (DH_MEM ** -0.5)
            p = jnp.exp(s - jnp.max(s, axis=-1, keepdims=True))
            p = p / jnp.sum(p, axis=-1, keepdims=True)
            outs.append(_dot(p.astype(dt), mv_ref[bb, :, cols].astype(dt)))
        o_ref[bb * tm:(bb + 1) * tm, :] = jnp.concatenate(outs, axis=1)


def _mem_attn(p, mk, mv, q_gain, b, t):
    tm = min(512, t)
    nt = t // tm
    nb = 8 if (nt == 1 and tm < 64 and b % 8 == 0) else 1
    n_mem = mk.shape[1]
    kv = pl.BlockSpec((nb, n_mem, MEM_W), lambda i, j: (i, 0, 0))
    return pl.pallas_call(
        functools.partial(_mem_attn_kernel, cast=tm >= 16, nb=nb),
        grid=(b // nb, nt),
        in_specs=[pl.BlockSpec((nb * tm, MEM_W), lambda i, j: (i * nt + j, MQ)), kv, kv,
                  pl.BlockSpec((1, LANES), lambda i, j: (0, 0))],
        out_specs=pl.BlockSpec((nb * tm, MEM_W), lambda i, j: (i * nt + j, 0)),
        out_shape=jax.ShapeDtypeStruct((b * t, MEM_W), F32),
        compiler_params=_params("parallel", "parallel"),
        name="mem_attn")(p, mk, mv, q_gain)


def _out_kernel(of_ref, og_ref, om_ref, fz_ref, gz_ref, mz_ref, g0_ref, g1_ref, g2_ref, x_ref,
                wf_ref, wg_ref, wm_ref, wo_ref, gg_ref, y_ref):
    f32 = lambda ref: ref[...].astype(F32)
    a = (of_ref[...] * _silu(fz_ref[...])).astype(BF16)
    b = (_headnorm128(og_ref[...], gg_ref[...], H_GDN) * _silu(f32(gz_ref))).astype(BF16)
    c = (om_ref[...] * _silu(f32(mz_ref))).astype(BF16)
    merged = (_sigmoid(f32(g0_ref)) * _dot(a, wf_ref[...]) + _sigmoid(f32(g1_ref)) * _dot(b, wg_ref[...])
              + _sigmoid(f32(g2_ref)) * _dot(c, wm_ref[...]))
    y_ref[...] = x_ref[...] + _dot(merged.astype(BF16), wo_ref[...])


def _out_proj(o_fox, o_gdn, o_mem, p32, p16, x, wf, wg, wm, wo, gdn_gain):
    n, d = x.shape
    tm = min(512, n)
    branch = pl.BlockSpec((tm, FOX_W), lambda i: (i, 0))
    pz = lambda blk: pl.BlockSpec((tm, FOX_W), lambda i, blk=blk: (i, blk))
    pg = lambda blk: pl.BlockSpec((tm, d), lambda i, blk=blk: (i, blk))
    const = lambda shape: pl.BlockSpec(shape, lambda i: (0, 0))
    return pl.pallas_call(
        _out_kernel,
        grid=(n // tm,),
        in_specs=[branch, branch, branch, pz(FZ), pz(GZ), pz(MZ), pg(GATE0), pg(GATE0 + 1), pg(GATE0 + 2),
                  pl.BlockSpec((tm, d), lambda i: (i, 0)),
                  const(wf.shape), const(wg.shape), const(wm.shape), const(wo.shape), const((1, LANES))],
        out_specs=pl.BlockSpec((tm, d), lambda i: (i, 0)),
        out_shape=jax.ShapeDtypeStruct((n, d), F32),
        compiler_params=_params("parallel"),
        name="out_proj")(o_fox, o_gdn, o_mem, p32, p16, p16, p16, p16, p16, x, wf, wg, wm, wo, gdn_gain)


def _split_w_in(w):
    o = np.cumsum([0, FOX_W, FOX_W, FOX_W, FOX_W, H_FOX, 3 * GDN_W, GDN_W, H_GDN, H_GDN, MEM_W, MEM_W])
    big = jnp.concatenate([w[:, o[0]:o[3]], w[:, o[5]:o[6]], w[:, o[9]:o[10]], w[:, o[3]:o[4]],
                           w[:, o[6]:o[7]], w[:, o[10]:o[11]], w[:, o[11]:]], axis=1).astype(BF16)
    assert big.shape[1] == 2 * P_HALF
    tn = 1024
    big = big.reshape(big.shape[0], big.shape[1] // tn, tn).transpose(1, 0, 2)
    small = jnp.concatenate([w[:, o[4]:o[5]], w[:, o[7]:o[9]]], axis=1)
    small = jnp.pad(small, ((0, 0), (0, LANES - small.shape[1]))).astype(BF16)
    return big, small


def kernel(x_prompt, x_sample, cache_fox_k, cache_fox_v, cache_fox_logf, state_gdn, state_gdn_conv,
           cache_mem_k, cache_mem_v, page_table, mem_prompt, ln_gain, w_in, fox_q_gain, fox_k_gain,
           fox_f_bias, gdn_conv_w, gdn_A_log, gdn_dt_bias, gdn_out_gain, mem_norm_gain, w_mem_kv,
           mem_q_gain, mem_k_gain, w_fox_br, w_gdn_br, w_mem_br, w_out):
    b, t, d = x_prompt.shape
    bd_, td, _ = x_sample.shape
    depth, n_pool, page = cache_fox_k.shape[:3]
    n_pages = page_table.shape[1]
    n_mem = mem_prompt.shape[1]
    c = GDN_CHUNK
    assert page == LANES and t % c == 0 and td <= c and td % 8 == 0

    kc = jnp.transpose(cache_fox_k, (0, 1, 3, 4, 2)).reshape(depth * n_pool, FOX_W, page)
    vc = jnp.transpose(cache_fox_v, (0, 1, 3, 4, 2)).reshape(depth * n_pool, FOX_W, page)
    fc = jnp.swapaxes(cache_fox_logf, 2, 3).reshape(depth * n_pool, H_FOX, page)

    yp = x_prompt.reshape(b * t, d)
    ys = x_sample.reshape(bd_ * td, d)
    eye = jnp.eye(H_FOX, dtype=F32)
    outs = [[] for _ in range(12)]
    for l in range(depth):
        w_big, w_small = _split_w_in(w_in[l])
        gain = ln_gain[l][None, :].astype(F32)
        wf, wg, wm, wo = [z[l].astype(BF16) for z in (w_fox_br, w_gdn_br, w_mem_br, w_out)]
        gdn_gain = gdn_out_gain[l][None, :].astype(F32)
        mq_gain = mem_q_gain[l][None, :].astype(F32)

        p, p16, ps = _in_proj(yp, gain, w_big, w_small, min(1024, b * t))
        knt, fvt, lf, qa, ka, va = _fox_prep_prompt(p, ps, fox_q_gain[l], fox_k_gain[l], fox_f_bias[l], b, t)
        kn, fv = [z.reshape(b, H_FOX, DH_FOX, t).transpose(0, 3, 1, 2) for z in (knt, fvt)]
        o_fox = _fox_attn_prompt(qa, ka, va, b, t).reshape(b * t, FOX_W)
        gq, gk, gv, gg = _gdn_prep(p, ps, jnp.zeros((b, 8, 3 * GDN_W), F32), gdn_conv_w[l], gdn_A_log[l],
                                   gdn_dt_bias[l], b, t)
        o_gdn, s_new = _gdn(gq, gk, gv, gg, jnp.zeros((b, H_GDN, DK_GDN, DV_GDN), F32), b, t)
        mk, mv = _mem_kv(mem_prompt.reshape(b * n_mem, d), mem_norm_gain[l][None, :].astype(F32),
                         w_mem_kv[l].astype(BF16), mem_k_gain[l][None, :].astype(F32))
        o_mem = _mem_attn(p, mk.reshape(b, n_mem, MEM_W), mv.reshape(b, n_mem, MEM_W), mq_gain, b, t)
        conv_tail = p.reshape(b, t, -1)[:, t - (CONV_W - 1):, GQ * FOX_W:GQ * FOX_W + 3 * GDN_W]
        yp = _out_proj(o_fox, o_gdn, o_mem, p, p16, yp, wf, wg, wm, wo, gdn_gain)
        for lst, val in zip(outs[:7], (kn, fv, lf[:, :, :H_FOX], s_new, conv_tail,
                                       mk.reshape(b, n_mem, H_MEM, DH_MEM), mv.reshape(b, n_mem, H_MEM, DH_MEM))):
            lst.append(val)

        p, p16, ps = _in_proj(ys, gain, w_big, w_small, bd_ * td)
        qn, kn, lf, cn = _fox_sample_prep(p, ps, fox_q_gain[l], fox_k_gain[l], fox_f_bias[l], td)
        fv = p[:, FV * FOX_W:(FV + 1) * FOX_W]
        q4 = qn.reshape(bd_, td, H_FOX, DH_FOX).transpose(0, 2, 1, 3)
        qbd = (q4[:, :, :, None, :] * eye[None, :, None, :, None]).reshape(bd_, H_FOX * td, FOX_W).astype(BF16)
        cn3 = cn.reshape(bd_, td, LANES)[:, :, :H_FOX].transpose(0, 2, 1)
        cnq = jnp.broadcast_to(cn3.reshape(bd_, H_FOX * td, 1), (bd_, H_FOX * td, LANES))
        cnk = jnp.broadcast_to(cn3[:, :, None, :], (bd_, H_FOX, td, td)).reshape(bd_, H_FOX * td, td)
        cnk = jnp.pad(cnk, ((0, 0), (0, 0), (0, LANES - td)))
        pad_rows = lambda z: jnp.pad(z.reshape(bd_, td, FOX_W), ((0, 0), (0, page - td), (0, 0))).astype(BF16).transpose(0, 2, 1)
        pidx = (l * n_pool + page_table).reshape(-1).astype(jnp.int32)
        o_fox = _fox_decode(pidx, qbd, cnq, cnk, pad_rows(kn), pad_rows(fv), kc, vc, fc, n_pages, td)
        o_fox = o_fox.reshape(bd_ * td, FOX_W)

        init = jnp.pad(state_gdn_conv[l].astype(F32), ((0, 0), (8 - (CONV_W - 1), 0), (0, 0)))
        gq, gk, gv, gg = _gdn_prep(p, ps, init, gdn_conv_w[l], gdn_A_log[l], gdn_dt_bias[l], bd_, td)
        padc = lambda z: jnp.pad(z.reshape(bd_, td, -1), ((0, 0), (0, c - td), (0, 0))).reshape(bd_ * c, -1)
        o_gdn, s_new = _gdn(padc(gq), padc(gk), padc(gv), padc(gg), state_gdn[l].astype(F32), bd_, c)
        o_gdn = o_gdn.reshape(bd_, c, GDN_W)[:, :td].reshape(bd_ * td, GDN_W)
        o_mem = _mem_attn(p, cache_mem_k[l].reshape(bd_, n_mem, MEM_W), cache_mem_v[l].reshape(bd_, n_mem, MEM_W),
                          mq_gain, bd_, td)
        conv_tail = jnp.concatenate([state_gdn_conv[l].astype(F32),
                                     p.reshape(bd_, td, -1)[:, :, GQ * FOX_W:GQ * FOX_W + 3 * GDN_W]],
                                    axis=1)[:, td:]
        ys = _out_proj(o_fox, o_gdn, o_mem, p, p16, ys, wf, wg, wm, wo, gdn_gain)
        for lst, val in zip(outs[7:], (kn.reshape(bd_, td, H_FOX, DH_FOX), fv.reshape(bd_, td, H_FOX, DH_FOX),
                                       lf.reshape(bd_, td, LANES)[:, :, :H_FOX], s_new, conv_tail)):
            lst.append(val)

    return (yp.reshape(b, t, d), ys.reshape(bd_, td, d)) + tuple(jnp.stack(z) for z in outs)
```
